```python
import math
import jax, jax.numpy as jnp
from jax import lax
import numpy as np

D_MODEL = 2048
BATCH = 2
SEQ = 8192
DEPTH = 4

N_MIXERS = 3
N_LAYERS_A = (DEPTH + 2) // 3
N_LAYERS_B = (DEPTH + 1) // 3
N_LAYERS_C = DEPTH // 3
HEAD_DIM = 128
D_FF = 5632
RMS_EPS = 1e-6
Q_BLOCK = 128
A_HEADS = D_MODEL // (2 * HEAD_DIM)
A_VDIM = 2 * HEAD_DIM
B_HEADS = D_MODEL // HEAD_DIM
GRID_W = 64
NA_ROWS = 8
NA_COLS = 16
C_HEADS = D_MODEL // HEAD_DIM
C_KV_HEADS = 4
C_GROUP = C_HEADS // C_KV_HEADS
C_WINDOW = 128

kernel_name = "hybrid_macaron_diffattn_natten_swa_encoder"


def rms_norm(x, g):
    x32 = x.astype(jnp.float32)
    y = x32 * lax.rsqrt(jnp.mean(x32 * x32, axis=-1, keepdims=True) + RMS_EPS)
    return (y * g.astype(jnp.float32)).astype(x.dtype)


def swiglu(x, w_gate, w_up, w_down):
    return (jax.nn.silu(x @ w_gate) * (x @ w_up)) @ w_down


def alibi_slopes(n_heads):
    return jnp.asarray(2.0 ** (-8.0 * np.arange(1, n_heads + 1) / n_heads), dtype=jnp.float32)


def diff_attention(h, w_qkv, q_gain, k_gain, lq1, lk1, lq2, lk2, subln_g, w_o, lambda_init):
    B, S, _ = h.shape
    qk_w = A_HEADS * 2 * HEAD_DIM
    q, k, v = jnp.split(h @ w_qkv, [qk_w, 2 * qk_w], axis=-1)
    q = rms_norm(q.reshape(B, S, A_HEADS, 2, HEAD_DIM), q_gain)
    k = rms_norm(k.reshape(B, S, A_HEADS, 2, HEAD_DIM), k_gain)
    v = v.reshape(B, S, A_HEADS, A_VDIM)
    lam = (jnp.exp(jnp.sum(lq1.astype(jnp.float32) * lk1.astype(jnp.float32)))
           - jnp.exp(jnp.sum(lq2.astype(jnp.float32) * lk2.astype(jnp.float32))) + lambda_init)
    slopes = alibi_slopes(A_HEADS)
    pos = jnp.arange(S)
    n_blk = S // Q_BLOCK
    scale = HEAD_DIM ** -0.5
    q_blocks = q.reshape(B, n_blk, Q_BLOCK, A_HEADS, 2, HEAD_DIM).transpose(1, 0, 2, 3, 4, 5)

    def one_block(args):
        qb, t0 = args
        t = t0 + jnp.arange(Q_BLOCK)
        dist = jnp.abs(t[:, None] - pos[None, :]).astype(jnp.float32)
        s = (jnp.einsum('bqhcd,bkhcd->bhcqk', qb, k).astype(jnp.float32) * scale
             - slopes[:, None, None, None] * dist)
        p = jax.nn.softmax(s, axis=-1)
        a = p[:, :, 0] - lam * p[:, :, 1]
        return jnp.einsum('bhqk,bkhe->bqhe', a.astype(v.dtype), v)

    o = lax.map(one_block, (q_blocks, jnp.arange(n_blk) * Q_BLOCK))
    o = o.transpose(1, 0, 2, 3, 4).reshape(B, S, A_HEADS, A_VDIM)
    o = rms_norm(o, subln_g) * (1.0 - lambda_init)
    return o.reshape(B, S, A_HEADS * A_VDIM) @ w_o


def neighbourhood_attention(h, w_qkv, q_gain, k_gain, rel_bias, w_o):
    B, S, _ = h.shape
    rows = S // GRID_W
    kr = min(NA_ROWS, rows)
    kc = NA_COLS
    q, k, v = jnp.split(h @ w_qkv, 3, axis=-1)
    q = rms_norm(q.reshape(B, rows, GRID_W, B_HEADS, HEAD_DIM), q_gain)
    k = rms_norm(k.reshape(B, rows, GRID_W, B_HEADS, HEAD_DIM), k_gain)
    v = v.reshape(B, rows, GRID_W, B_HEADS, HEAD_DIM)
    cols = np.arange(GRID_W)
    c_start = np.clip(cols - kc // 2, 0, GRID_W - kc)
    col_idx = c_start[:, None] + np.arange(kc)
    dcol = col_idx - cols[:, None] + (NA_COLS - 1)
    bias_cols = rel_bias[:, :, dcol]
    scale = HEAD_DIM ** -0.5

    def one_row(r):
        r_start = jnp.clip(r - kr // 2, 0, rows - kr)
        k_win = lax.dynamic_slice_in_dim(k, r_start, kr, axis=1)[:, :, col_idx]
        v_win = lax.dynamic_slice_in_dim(v, r_start, kr, axis=1)[:, :, col_idx]
        q_r = lax.dynamic_index_in_dim(q, r, axis=1, keepdims=False)
        drow = r_start + jnp.arange(kr) - r + (NA_ROWS - 1)
        bias = jnp.take(bias_cols, drow, axis=1).transpose(0, 2, 1, 3)
        s = (jnp.einsum('bchd,brckhd->bhcrk', q_r, k_win).astype(jnp.float32) * scale
             + bias[None].astype(jnp.float32))
        p = jax.nn.softmax(s.reshape(B, B_HEADS, GRID_W, kr * kc), axis=-1)
        p = p.reshape(B, B_HEADS, GRID_W, kr, kc)
        return jnp.einsum('bhcrk,brckhd->bchd', p.astype(v.dtype), v_win)

    o = lax.map(one_row, jnp.arange(rows))
    o = o.transpose(1, 0, 2, 3, 4).reshape(B, S, B_HEADS * HEAD_DIM)
    return o @ w_o


def window_gqa(h, w_qkv, q_gain, k_gain, sink, w_o):
    B, S, _ = h.shape
    q_w = C_HEADS * HEAD_DIM
    kv_w = C_KV_HEADS * HEAD_DIM
    q, k, v = jnp.split(h @ w_qkv, [q_w, q_w + kv_w], axis=-1)
    n_blk = S // Q_BLOCK
    q = rms_norm(q.reshape(B, n_blk, Q_BLOCK, C_KV_HEADS, C_GROUP, HEAD_DIM), q_gain)
    k = rms_norm(k.reshape(B, S, C_KV_HEADS, HEAD_DIM), k_gain)
    v = v.reshape(B, S, C_KV_HEADS, HEAD_DIM)
    pad = ((0, 0), (Q_BLOCK, Q_BLOCK), (0, 0), (0, 0))
    kp = jnp.pad(k, pad).reshape(B, n_blk + 2, Q_BLOCK, C_KV_HEADS, HEAD_DIM)
    vp = jnp.pad(v, pad).reshape(B, n_blk + 2, Q_BLOCK, C_KV_HEADS, HEAD_DIM)
    k_band = jnp.concatenate([kp[:, :-2], kp[:, 1:-1], kp[:, 2:]], axis=2)
    v_band = jnp.concatenate([vp[:, :-2], vp[:, 1:-1], vp[:, 2:]], axis=2)
    blk0 = jnp.arange(n_blk)[:, None] * Q_BLOCK
    t = blk0 + jnp.arange(Q_BLOCK)
    s_pos = blk0 - Q_BLOCK + jnp.arange(3 * Q_BLOCK)
    dist = jnp.abs(t[:, :, None] - s_pos[:, None, :])
    valid = (dist <= C_WINDOW) & (s_pos[:, None, :] >= 0) & (s_pos[:, None, :] < S)
    slopes = alibi_slopes(C_HEADS).reshape(C_KV_HEADS, C_GROUP)
    scale = HEAD_DIM ** -0.5
    s = (jnp.einsum('bnqkgd,bnskd->bkgnqs', q, k_band).astype(jnp.float32) * scale
         - slopes[:, :, None, None, None] * dist.astype(jnp.float32))
    s = jnp.where(valid, s, -jnp.inf)
    sink_l = jnp.broadcast_to(sink.astype(jnp.float32).reshape(C_KV_HEADS, C_GROUP)[None, :, :, None, None, None],
                              s.shape[:-1] + (1,))
    p = jax.nn.softmax(jnp.concatenate([s, sink_l], axis=-1), axis=-1)[..., :-1]
    o = jnp.einsum('bkgnqs,bnskd->bnqkgd', p.astype(v.dtype), v_band)
    return o.reshape(B, S, C_HEADS * HEAD_DIM) @ w_o


def setup_inputs(seed: int = 0) -> dict:
    key = jax.random.key(seed)
    keys = iter(jax.random.split(key, 40))
    f32 = jnp.float32

    def w(shape, fan_in):
        return jax.random.normal(next(keys), shape, f32) * (fan_in ** -0.5)

    def gain(shape):
        return 1.0 + 0.02 * jax.random.normal(next(keys), shape, f32)

    def small(shape, s):
        return s * jax.random.normal(next(keys), shape, f32)

    return {
        "x": jax.random.normal(next(keys), (BATCH, SEQ, D_MODEL), f32),
        "norm_ffn1": gain((DEPTH, D_MODEL)),
        "ffn1_w_gate": w((DEPTH, D_MODEL, D_FF), D_MODEL),
        "ffn1_w_up": w((DEPTH, D_MODEL, D_FF), D_MODEL),
        "ffn1_w_down": w((DEPTH, D_FF, D_MODEL), D_FF),
        "norm_mix": gain((DEPTH, D_MODEL)),
        "norm_ffn2": gain((DEPTH, D_MODEL)),
        "ffn2_w_gate": w((DEPTH, D_MODEL, D_FF), D_MODEL),
        "ffn2_w_up": w((DEPTH, D_MODEL, D_FF), D_MODEL),
        "ffn2_w_down": w((DEPTH, D_FF, D_MODEL), D_FF),
        "a_w_qkv": w((N_LAYERS_A, D_MODEL, 2 * A_HEADS * 2 * HEAD_DIM + A_HEADS * A_VDIM), D_MODEL),
        "a_q_norm": gain((N_LAYERS_A, HEAD_DIM)),
        "a_k_norm": gain((N_LAYERS_A, HEAD_DIM)),
        "a_lambda_q1": small((N_LAYERS_A, HEAD_DIM), 0.1),
        "a_lambda_k1": small((N_LAYERS_A, HEAD_DIM), 0.1),
        "a_lambda_q2": small((N_LAYERS_A, HEAD_DIM), 0.1),
        "a_lambda_k2": small((N_LAYERS_A, HEAD_DIM), 0.1),
        "a_subln": gain((N_LAYERS_A, A_VDIM)),
        "a_w_o": w((N_LAYERS_A, A_HEADS * A_VDIM, D_MODEL), A_HEADS * A_VDIM),
        "b_w_qkv": w((N_LAYERS_B, D_MODEL, 3 * B_HEADS * HEAD_DIM), D_MODEL),
        "b_q_norm": gain((N_LAYERS_B, HEAD_DIM)),
        "b_k_norm": gain((N_LAYERS_B, HEAD_DIM)),
        "b_rel_bias": small((N_LAYERS_B, B_HEADS, 2 * NA_ROWS - 1, 2 * NA_COLS - 1), 0.5),
        "b_w_o": w((N_LAYERS_B, B_HEADS * HEAD_DIM, D_MODEL), B_HEADS * HEAD_DIM),
        "c_w_qkv": w((N_LAYERS_C, D_MODEL, (C_HEADS + 2 * C_KV_HEADS) * HEAD_DIM), D_MODEL),
        "c_q_norm": gain((N_LAYERS_C, HEAD_DIM)),
        "c_k_norm": gain((N_LAYERS_C, HEAD_DIM)),
        "c_sink": small((N_LAYERS_C, C_HEADS), 1.0),
        "c_w_o": w((N_LAYERS_C, C_HEADS * HEAD_DIM, D_MODEL), C_HEADS * HEAD_DIM),
    }


def reference(x, norm_ffn1, ffn1_w_gate, ffn1_w_up, ffn1_w_down, norm_mix, norm_ffn2,
              ffn2_w_gate, ffn2_w_up, ffn2_w_down,
              a_w_qkv, a_q_norm, a_k_norm, a_lambda_q1, a_lambda_k1, a_lambda_q2, a_lambda_k2,
              a_subln, a_w_o,
              b_w_qkv, b_q_norm, b_k_norm, b_rel_bias, b_w_o,
              c_w_qkv, c_q_norm, c_k_norm, c_sink, c_w_o):
    for i in range(DEPTH):
        x = x + 0.5 * swiglu(rms_norm(x, norm_ffn1[i]), ffn1_w_gate[i], ffn1_w_up[i], ffn1_w_down[i])
        h = rms_norm(x, norm_mix[i])
        kind, j = i % N_MIXERS, i // N_MIXERS
        if kind == 0:
            lambda_init = 0.8 - 0.6 * math.exp(-0.3 * i)
            y = diff_attention(h, a_w_qkv[j], a_q_norm[j], a_k_norm[j], a_lambda_q1[j], a_lambda_k1[j],
                               a_lambda_q2[j], a_lambda_k2[j], a_subln[j], a_w_o[j], lambda_init)
        elif kind == 1:
            y = neighbourhood_attention(h, b_w_qkv[j], b_q_norm[j], b_k_norm[j], b_rel_bias[j], b_w_o[j])
        else:
            y = window_gqa(h, c_w_qkv[j], c_q_norm[j], c_k_norm[j], c_sink[j], c_w_o[j])
        x = x + y
        x = x + 0.5 * swiglu(rms_norm(x, norm_ffn2[i]), ffn2_w_gate[i], ffn2_w_up[i], ffn2_w_down[i])
    return x
```

```python
import functools
import math

import jax
import jax.numpy as jnp
import numpy as np
from jax import lax
from jax.experimental import pallas as pl
from jax.experimental.pallas import tpu as pltpu

D_MODEL = 2048
DEPTH = 4
N_MIXERS = 3
HEAD_DIM = 128
D_FF = 5632
RMS_EPS = 1e-6
A_HEADS = D_MODEL // (2 * HEAD_DIM)
A_VDIM = 2 * HEAD_DIM
B_HEADS = D_MODEL // HEAD_DIM
GRID_W = 64
NA_ROWS = 8
NA_COLS = 16
C_HEADS = D_MODEL // HEAD_DIM
C_KV_HEADS = 4
C_GROUP = C_HEADS // C_KV_HEADS
C_WINDOW = 128

LOG2E = math.log2(math.e)
NEG_BIG = -1e30
BF16 = jnp.bfloat16
F32 = jnp.float32

V7X_VMEM_BYTES = 64 * 1024 * 1024
VMEM_LIMIT = 56 * 1024 * 1024
LANE = 128


def _params(semantics):
    return pltpu.CompilerParams(dimension_semantics=semantics, vmem_limit_bytes=VMEM_LIMIT)


def _dot(a, b):
    return jnp.dot(a, b, preferred_element_type=F32)


def _dot_nt(a, b):
    return lax.dot_general(a, b, (((1,), (1,)), ((), ())), preferred_element_type=F32)


def _rms_normalise(x):
    return x * lax.rsqrt(jnp.mean(x * x, axis=-1, keepdims=True) + RMS_EPS)


FFN_TM = 512
FFN_TF = 512


def _ffn_kernel(x_ref, g_ref, wg_ref, wu_ref, wd_ref, o_ref, h_ref):
    j = pl.program_id(1)

    @pl.when(j == 0)
    def _():
        x = x_ref[...]
        h_ref[...] = (_rms_normalise(x) * g_ref[...]).astype(BF16)
        o_ref[...] = x

    h = h_ref[...]
    gate = _dot(h, wg_ref[...])
    up = _dot(h, wu_ref[...])
    act = gate * (0.5 / (1.0 + jnp.exp(-gate))) * up
    o_ref[...] += _dot(act.astype(BF16), wd_ref[...])


def _ffn(x, g, wg, wu, wd):
    m, d = x.shape
    f = wg.shape[1]
    return pl.pallas_call(
        _ffn_kernel,
        grid=(m // FFN_TM, f // FFN_TF),
        in_specs=[
            pl.BlockSpec((FFN_TM, d), lambda i, j: (i, 0)),
            pl.BlockSpec((1, d), lambda i, j: (0, 0)),
            pl.BlockSpec((d, FFN_TF), lambda i, j: (0, j)),
            pl.BlockSpec((d, FFN_TF), lambda i, j: (0, j)),
            pl.BlockSpec((FFN_TF, d), lambda i, j: (j, 0)),
        ],
        out_specs=pl.BlockSpec((FFN_TM, d), lambda i, j: (i, 0)),
        out_shape=jax.ShapeDtypeStruct((m, d), F32),
        scratch_shapes=[pltpu.VMEM((FFN_TM, d), BF16)],
        compiler_params=_params(("parallel", "arbitrary")),
        name="macaron_ffn",
    )(x, g.reshape(1, d), wg, wu, wd)


PROJ_TM = 1024
PROJ_TN = 512


def _qkv_kernel(x_ref, g_ref, w_ref, cg_ref, o_ref, h_ref, *, n_norm_blocks):
    j = pl.program_id(1)

    @pl.when(j == 0)
    def _():
        h_ref[...] = (_rms_normalise(x_ref[...]) * g_ref[...]).astype(BF16)

    y = _dot(h_ref[...], w_ref[...])

    @pl.when(j < n_norm_blocks)
    def _():
        for c in range(PROJ_TN // HEAD_DIM):
            cols = slice(c * HEAD_DIM, (c + 1) * HEAD_DIM)
            o_ref[:, cols] = (_rms_normalise(y[:, cols]) * cg_ref[:, cols]).astype(BF16)

    @pl.when(j >= n_norm_blocks)
    def _():
        o_ref[...] = y.astype(BF16)


def _qkv(x, g, w, col_gain, n_norm_cols):
    m, d = x.shape
    n = w.shape[1]
    kern = functools.partial(_qkv_kernel, n_norm_blocks=n_norm_cols // PROJ_TN)
    return pl.pallas_call(
        kern,
        grid=(m // PROJ_TM, n // PROJ_TN),
        in_specs=[
            pl.BlockSpec((PROJ_TM, d), lambda i, j: (i, 0)),
            pl.BlockSpec((1, d), lambda i, j: (0, 0)),
            pl.BlockSpec((d, PROJ_TN), lambda i, j: (0, j)),
            pl.BlockSpec((1, PROJ_TN), lambda i, j: (0, j)),
        ],
        out_specs=pl.BlockSpec((PROJ_TM, PROJ_TN), lambda i, j: (i, j)),
        out_shape=jax.ShapeDtypeStruct((m, n), BF16),
        scratch_shapes=[pltpu.VMEM((PROJ_TM, d), BF16)],
        compiler_params=_params(("parallel", "arbitrary")),
        name="mixer_qkv",
    )(x, g.reshape(1, d), w, col_gain)


def _out_proj_kernel(a_ref, w_ref, x_ref, o_ref):
    o_ref[...] = x_ref[...] + _dot(a_ref[...], w_ref[...])


def _out_proj(a, w, x):
    m, k = a.shape
    n = w.shape[1]
    return pl.pallas_call(
        _out_proj_kernel,
        grid=(m // PROJ_TM, n // PROJ_TN),
        in_specs=[
            pl.BlockSpec((PROJ_TM, k), lambda i, j: (i, 0)),
            pl.BlockSpec((k, PROJ_TN), lambda i, j: (0, j)),
            pl.BlockSpec((PROJ_TM, PROJ_TN), lambda i, j: (i, j)),
        ],
        out_specs=pl.BlockSpec((PROJ_TM, PROJ_TN), lambda i, j: (i, j)),
        out_shape=jax.ShapeDtypeStruct((m, n), F32),
        compiler_params=_params(("parallel", "arbitrary")),
        name="mixer_out_proj",
    )(a, w, x)


A_TQ = 512
A_TK = 512


def _diff_attn_kernel(lam_ref, slope_ref, q_ref, k_ref, v_ref, sg_ref, o_ref,
                      cd_ref, m_ref, l_ref, acc_ref, *, out_scale):
    h = pl.program_id(1)
    qi = pl.program_id(2)
    n_kv = k_ref.shape[0] // A_TK
    c = slope_ref[h] * LOG2E

    rows = lax.broadcasted_iota(jnp.int32, (A_TQ, A_TK), 0)
    cols = lax.broadcasted_iota(jnp.int32, (A_TQ, A_TK), 1)
    cd_ref[...] = c * (rows - cols).astype(F32)
    m_ref[...] = jnp.full(m_ref.shape, NEG_BIG, F32)
    l_ref[...] = jnp.zeros(l_ref.shape, F32)
    acc_ref[...] = jnp.zeros(acc_ref.shape, F32)

    def block(j, mode):
        k0 = pl.multiple_of(j * A_TK, A_TK)
        kb = k_ref[pl.ds(k0, A_TK), :]
        vb = v_ref[pl.ds(k0, A_TK), :]
        off = c * (qi * A_TQ - k0).astype(F32)
        for half in range(2):
            cols_h = slice(half * HEAD_DIM, (half + 1) * HEAD_DIM)
            s = _dot_nt(q_ref[:, cols_h], kb[:, cols_h])
            if mode == "left":
                z, kappa = s - cd_ref[...], -off
            elif mode == "right":
                z, kappa = s + cd_ref[...], off
            else:
                z, kappa = s - jnp.abs(cd_ref[...] + off), 0.0
            m_old = m_ref[half]
            m_new = jnp.maximum(m_old, jnp.max(z, axis=-1, keepdims=True) + kappa)
            p = jnp.exp2(z - (m_new - kappa))
            alpha = jnp.exp2(m_old - m_new)
            l_ref[half] = alpha * l_ref[half] + jnp.sum(p, axis=-1, keepdims=True)
            acc_ref[half] = alpha * acc_ref[half] + _dot(p.astype(BF16), vb)
            m_ref[half] = m_new

    def left(j, carry):
        block(j, "left")
        return carry

    def right(j, carry):
        block(j, "right")
        return carry

    lax.fori_loop(0, qi, left, 0)
    block(qi, "diag")
    lax.fori_loop(qi + 1, n_kv, right, 0)

    o = acc_ref[0] / l_ref[0] - lam_ref[0] * (acc_ref[1] / l_ref[1])
    o_ref[...] = (_rms_normalise(o) * (sg_ref[...] * out_scale)).astype(BF16)


def _diff_attention(qkv, lam, subln_g, batch, seq, lambda_init):
    assert A_TQ == A_TK
    m = qkv.shape[0]
    n_q = seq // A_TQ
    slopes = jnp.asarray(2.0 ** (-8.0 * np.arange(1, A_HEADS + 1) / A_HEADS), F32)
    kern = functools.partial(_diff_attn_kernel, out_scale=1.0 - lambda_init)
    smem = pl.BlockSpec(memory_space=pltpu.SMEM)
    return pl.pallas_call(
        kern,
        grid=(batch, A_HEADS, n_q),
        in_specs=[
            smem,
            smem,
            pl.BlockSpec((A_TQ, A_VDIM), lambda b, h, i: (b * n_q + i, h)),
            pl.BlockSpec((seq, A_VDIM), lambda b, h, i: (b, A_HEADS + h)),
            pl.BlockSpec((seq, A_VDIM), lambda b, h, i: (b, 2 * A_HEADS + h)),
            pl.BlockSpec((1, A_VDIM), lambda b, h, i: (0, 0)),
        ],
        out_specs=pl.BlockSpec((A_TQ, A_VDIM), lambda b, h, i: (b * n_q + i, h)),
        out_shape=jax.ShapeDtypeStruct((m, A_HEADS * A_VDIM), BF16),
        scratch_shapes=[
            pltpu.VMEM((A_TQ, A_TK), F32),
            pltpu.VMEM((2, A_TQ, 1), F32),
            pltpu.VMEM((2, A_TQ, 1), F32),
            pltpu.VMEM((2, A_TQ, A_VDIM), F32),
        ],
        compiler_params=_params(("parallel", "parallel", "arbitrary")),
        name="diff_attention",
    )(lam.reshape(1), slopes, qkv, qkv, qkv, subln_g.reshape(1, A_VDIM))


B_QROWS = 8
B_KROWS = 16
B_TQ = B_QROWS * GRID_W
B_TK = B_KROWS * GRID_W


def _na_bias_table(rel_bias, rows):
    kr = min(NA_ROWS, rows)
    n_groups = rows // B_QROWS
    c = np.arange(GRID_W)
    c_start = np.clip(c - NA_COLS // 2, 0, GRID_W - NA_COLS)
    kc = np.arange(GRID_W)
    col_ok = (kc[None, :] >= c_start[:, None]) & (kc[None, :] < c_start[:, None] + NA_COLS)
    dcol = np.clip(kc[None, :] - c[:, None] + (NA_COLS - 1), 0, 2 * NA_COLS - 2)
    tables = []
    for g in (0, 1, n_groups - 1):
        r = g * B_QROWS + np.arange(B_QROWS)
        k_row0 = int(np.clip(g * B_QROWS - kr // 2, 0, rows - B_KROWS))
        k_rows = k_row0 + np.arange(B_KROWS)
        r_start = np.clip(r - kr // 2, 0, rows - kr)
        row_ok = (k_rows[None, :] >= r_start[:, None]) & (k_rows[None, :] < r_start[:, None] + kr)
        drow = np.clip(k_rows[None, :] - r[:, None] + (NA_ROWS - 1), 0, 2 * NA_ROWS - 2)
        ok = row_ok[:, None, :, None] & col_ok[None, :, None, :]
        vals = rel_bias[:, drow[:, None, :, None], dcol[None, :, None, :]]
        tab = jnp.where(ok[None], vals * LOG2E, NEG_BIG)
        tables.append(tab.reshape(rel_bias.shape[0], B_TQ, B_TK))
    return jnp.stack(tables, axis=1).astype(F32)


def _na_kernel(q_ref, k_ref, v_ref, bias_ref, o_ref, *, rows):
    n_groups = rows // B_QROWS

    def group(g, carry):
        q0 = pl.multiple_of(g * B_TQ, B_TQ)
        k_row0 = jnp.clip(g * B_QROWS - NA_ROWS // 2, 0, rows - B_KROWS)
        k0 = pl.multiple_of(k_row0 * GRID_W, (NA_ROWS // 2) * GRID_W)
        variant = jnp.where(g == 0, 0, jnp.where(g == n_groups - 1, 2, 1))
        s = _dot_nt(q_ref[pl.ds(q0, B_TQ), :], k_ref[pl.ds(k0, B_TK), :]) + bias_ref[0, variant]
        p = jnp.exp2(s - jnp.max(s, axis=-1, keepdims=True))
        l = jnp.sum(p, axis=-1, keepdims=True)
        o = _dot(p.astype(BF16), v_ref[pl.ds(k0, B_TK), :])
        o_ref[pl.ds(q0, B_TQ), :] = (o / l).astype(BF16)
        return carry

    lax.fori_loop(0, n_groups, group, 0)


def _neighbourhood_attention(qkv, bias, batch, seq):
    m = qkv.shape[0]
    rows = seq // GRID_W
    assert rows % B_QROWS == 0 and rows >= B_KROWS and NA_ROWS <= rows
    kern = functools.partial(_na_kernel, rows=rows)
    return pl.pallas_call(
        kern,
        grid=(batch, B_HEADS),
        in_specs=[
            pl.BlockSpec((seq, HEAD_DIM), lambda b, h: (b, h)),
            pl.BlockSpec((seq, HEAD_DIM), lambda b, h: (b, B_HEADS + h)),
            pl.BlockSpec((seq, HEAD_DIM), lambda b, h: (b, 2 * B_HEADS + h)),
            pl.BlockSpec((1, 3, B_TQ, B_TK), lambda b, h: (h, 0, 0, 0)),
        ],
        out_specs=pl.BlockSpec((seq, HEAD_DIM), lambda b, h: (b, h)),
        out_shape=jax.ShapeDtypeStruct((m, B_HEADS * HEAD_DIM), BF16),
        compiler_params=_params(("parallel", "arbitrary")),
        name="neighbourhood_attention",
    )(qkv, qkv, qkv, bias)


C_TQ = 256
C_TK = C_TQ + 2 * C_WINDOW


def _swa_kernel(slope_ref, sink_ref, q_ref, k_ref, v_ref, o_ref):
    kv = pl.program_id(1)
    seq = k_ref.shape[0]
    rows = lax.broadcasted_iota(jnp.int32, (C_TQ, C_TK), 0)
    cols = lax.broadcasted_iota(jnp.int32, (C_TQ, C_TK), 1)
    delta = rows - cols

    def tile(i, carry):
        t0 = pl.multiple_of(i * C_TQ, C_TQ)
        k0 = pl.multiple_of(jnp.clip(t0 - C_WINDOW, 0, seq - C_TK), C_WINDOW)
        dist = jnp.abs(delta + (t0 - k0))
        valid = dist <= C_WINDOW
        dist_f = dist.astype(F32)
        kb = k_ref[pl.ds(k0, C_TK), :]
        vb = v_ref[pl.ds(k0, C_TK), :]
        for g in range(C_GROUP):
            head = kv * C_GROUP + g
            cols_g = slice(g * HEAD_DIM, (g + 1) * HEAD_DIM)
            s = _dot_nt(q_ref[pl.ds(t0, C_TQ), cols_g], kb)
            s = jnp.where(valid, s - (slope_ref[head] * LOG2E) * dist_f, NEG_BIG)
            sink = sink_ref[head] * LOG2E
            m = jnp.maximum(jnp.max(s, axis=-1, keepdims=True), sink)
            p = jnp.exp2(s - m)
            l = jnp.sum(p, axis=-1, keepdims=True) + jnp.exp2(sink - m)
            o = _dot(p.astype(BF16), vb)
            o_ref[pl.ds(t0, C_TQ), cols_g] = (o / l).astype(BF16)
        return carry

    lax.fori_loop(0, seq // C_TQ, tile, 0)


def _window_attention(qkv, sink, batch, seq):
    m = qkv.shape[0]
    assert seq % C_TQ == 0 and seq >= C_TK
    slopes = jnp.asarray(2.0 ** (-8.0 * np.arange(1, C_HEADS + 1) / C_HEADS), F32)
    gw = C_GROUP * HEAD_DIM
    smem = pl.BlockSpec(memory_space=pltpu.SMEM)
    return pl.pallas_call(
        _swa_kernel,
        grid=(batch, C_KV_HEADS),
        in_specs=[
            smem,
            smem,
            pl.BlockSpec((seq, gw), lambda b, kv: (b, kv)),
            pl.BlockSpec((seq, HEAD_DIM), lambda b, kv: (b, C_HEADS + kv)),
            pl.BlockSpec((seq, HEAD_DIM), lambda b, kv: (b, C_HEADS + C_KV_HEADS + kv)),
        ],
        out_specs=pl.BlockSpec((seq, gw), lambda b, kv: (b, kv)),
        out_shape=jax.ShapeDtypeStruct((m, C_HEADS * HEAD_DIM), BF16),
        compiler_params=_params(("parallel", "arbitrary")),
        name="window_attention",
    )(slopes, sink.astype(F32), qkv, qkv, qkv)


def _col_gain(q_gain, k_gain, n_q_heads, n_k_heads, n_cols):
    q_scale = HEAD_DIM ** -0.5 * LOG2E
    parts = [jnp.tile(q_gain.astype(F32) * q_scale, n_q_heads), jnp.tile(k_gain.astype(F32), n_k_heads)]
    gain = jnp.concatenate(parts)
    return jnp.pad(gain, (0, n_cols - gain.shape[0]), constant_values=1.0).reshape(1, n_cols)


def kernel(x, norm_ffn1, ffn1_w_gate, ffn1_w_up, ffn1_w_down, norm_mix, norm_ffn2, ffn2_w_gate, ffn2_w_up, ffn2_w_down, a_w_qkv, a_q_norm, a_k_norm, a_lambda_q1, a_lambda_k1, a_lambda_q2, a_lambda_k2, a_subln, a_w_o, b_w_qkv, b_q_norm, b_k_norm, b_rel_bias, b_w_o, c_w_qkv, c_q_norm, c_k_norm, c_sink, c_w_o):
    batch, seq, d = x.shape
    x = x.reshape(batch * seq, d)
    to_bf16 = lambda w: w.astype(BF16)
    ffn1 = tuple(map(to_bf16, (ffn1_w_gate, ffn1_w_up, ffn1_w_down)))
    ffn2 = tuple(map(to_bf16, (ffn2_w_gate, ffn2_w_up, ffn2_w_down)))
    a_w_qkv, a_w_o, b_w_qkv, b_w_o, c_w_qkv, c_w_o = map(
        to_bf16, (a_w_qkv, a_w_o, b_w_qkv, b_w_o, c_w_qkv, c_w_o))

    for i in range(DEPTH):
        x = _ffn(x, norm_ffn1[i], ffn1[0][i], ffn1[1][i], ffn1[2][i])
        kind, j = i % N_MIXERS, i // N_MIXERS
        if kind == 0:
            lambda_init = 0.8 - 0.6 * math.exp(-0.3 * i)
            lam = (jnp.exp(jnp.sum(a_lambda_q1[j].astype(F32) * a_lambda_k1[j].astype(F32)))
                   - jnp.exp(jnp.sum(a_lambda_q2[j].astype(F32) * a_lambda_k2[j].astype(F32))) + lambda_init)
            n_qk = 2 * A_HEADS * 2 * HEAD_DIM
            gain = _col_gain(a_q_norm[j], a_k_norm[j], 2 * A_HEADS, 2 * A_HEADS, a_w_qkv.shape[2])
            qkv = _qkv(x, norm_mix[i], a_w_qkv[j], gain, n_qk)
            o = _diff_attention(qkv, lam, a_subln[j], batch, seq, lambda_init)
            x = _out_proj(o, a_w_o[j], x)
        elif kind == 1:
            n_qk = 2 * B_HEADS * HEAD_DIM
            gain = _col_gain(b_q_norm[j], b_k_norm[j], B_HEADS, B_HEADS, b_w_qkv.shape[2])
            qkv = _qkv(x, norm_mix[i], b_w_qkv[j], gain, n_qk)
            bias = _na_bias_table(b_rel_bias[j].astype(F32), seq // GRID_W)
            o = _neighbourhood_attention(qkv, bias, batch, seq)
            x = _out_proj(o, b_w_o[j], x)
        else:
            n_qk = (C_HEADS + C_KV_HEADS) * HEAD_DIM
            gain = _col_gain(c_q_norm[j], c_k_norm[j], C_HEADS, C_KV_HEADS, c_w_qkv.shape[2])
            qkv = _qkv(x, norm_mix[i], c_w_qkv[j], gain, n_qk)
            o = _window_attention(qkv, c_sink[j], batch, seq)
            x = _out_proj(o, c_w_o[j], x)
        x = _ffn(x, norm_ffn2[i], ffn2[0][i], ffn2[1][i], ffn2[2][i])
    return x.reshape(batch, seq, d)
```

```python
import functools
import math

import jax
import jax.numpy as jnp
import numpy as np
from jax import lax
from jax.experimental import pallas as pl
from jax.experimental.pallas import tpu as pltpu

D_MODEL = 2048
DEPTH = 4
N_MIXERS = 3
HEAD_DIM = 128
D_FF = 5632
RMS_EPS = 1e-6
A_HEADS = D_MODEL // (2 * HEAD_DIM)
A_VDIM = 2 * HEAD_DIM
B_HEADS = D_MODEL // HEAD_DIM
GRID_W = 64
NA_ROWS = 8
NA_COLS = 16
C_HEADS = D_MODEL // HEAD_DIM
C_KV_HEADS = 4
C_GROUP = C_HEADS // C_KV_HEADS
C_WINDOW = 128

LOG2E = math.log2(math.e)
NEG_BIG = -1e30
BF16 = jnp.bfloat16
F32 = jnp.float32

V7X_VMEM_BYTES = 64 * 1024 * 1024
VMEM_LIMIT = 56 * 1024 * 1024
LANE = 128


def _params(semantics):
    return pltpu.CompilerParams(dimension_semantics=semantics, vmem_limit_bytes=VMEM_LIMIT)


def _dot(a, b):
    return jnp.dot(a, b, preferred_element_type=F32)


def _dot_nt(a, b):
    return lax.dot_general(a, b, (((1,), (1,)), ((), ())), preferred_element_type=F32)


def _rms_normalise(x):
    return x * lax.rsqrt(jnp.mean(x * x, axis=-1, keepdims=True) + RMS_EPS)


FFN_TM = 1024
FFN_TF = 512


def _ffn_kernel(x_ref, g_ref, wg_ref, wu_ref, wd_ref, o_ref, h_ref):
    j = pl.program_id(1)

    @pl.when(j == 0)
    def _():
        x = x_ref[...]
        h_ref[...] = (_rms_normalise(x) * g_ref[...]).astype(BF16)
        o_ref[...] = x

    h = h_ref[...]
    gate = _dot(h, wg_ref[...])
    up = _dot(h, wu_ref[...])
    act = gate * (0.5 / (1.0 + jnp.exp(-gate))) * up
    o_ref[...] += _dot(act.astype(BF16), wd_ref[...])


def _ffn(x, g, wg, wu, wd):
    m, d = x.shape
    f = wg.shape[1]
    return pl.pallas_call(
        _ffn_kernel,
        grid=(m // FFN_TM, f // FFN_TF),
        in_specs=[
            pl.BlockSpec((FFN_TM, d), lambda i, j: (i, 0)),
            pl.BlockSpec((1, d), lambda i, j: (0, 0)),
            pl.BlockSpec((d, FFN_TF), lambda i, j: (0, j)),
            pl.BlockSpec((d, FFN_TF), lambda i, j: (0, j)),
            pl.BlockSpec((FFN_TF, d), lambda i, j: (j, 0)),
        ],
        out_specs=pl.BlockSpec((FFN_TM, d), lambda i, j: (i, 0)),
        out_shape=jax.ShapeDtypeStruct((m, d), F32),
        scratch_shapes=[pltpu.VMEM((FFN_TM, d), BF16)],
        compiler_params=_params(("parallel", "arbitrary")),
        name="macaron_ffn",
    )(x, g.reshape(1, d), wg, wu, wd)


PROJ_TM = 1024
PROJ_TN = 512


def _qkv_kernel(x_ref, g_ref, w_ref, cg_ref, o_ref, h_ref, *, n_norm_blocks):
    j = pl.program_id(1)

    @pl.when(j == 0)
    def _():
        h_ref[...] = (_rms_normalise(x_ref[...]) * g_ref[...]).astype(BF16)

    y = _dot(h_ref[...], w_ref[...])

    @pl.when(j < n_norm_blocks)
    def _():
        for c in range(PROJ_TN // HEAD_DIM):
            cols = slice(c * HEAD_DIM, (c + 1) * HEAD_DIM)
            o_ref[:, cols] = (_rms_normalise(y[:, cols]) * cg_ref[:, cols]).astype(BF16)

    @pl.when(j >= n_norm_blocks)
    def _():
        o_ref[...] = y.astype(BF16)


def _qkv(x, g, w, col_gain, n_norm_cols, n=None):
    m, d = x.shape
    n = w.shape[1] if n is None else n
    kern = functools.partial(_qkv_kernel, n_norm_blocks=n_norm_cols // PROJ_TN)
    return pl.pallas_call(
        kern,
        grid=(m // PROJ_TM, n // PROJ_TN),
        in_specs=[
            pl.BlockSpec((PROJ_TM, d), lambda i, j: (i, 0)),
            pl.BlockSpec((1, d), lambda i, j: (0, 0)),
            pl.BlockSpec((d, PROJ_TN), lambda i, j: (0, j)),
            pl.BlockSpec((1, PROJ_TN), lambda i, j: (0, j)),
        ],
        out_specs=pl.BlockSpec((PROJ_TM, PROJ_TN), lambda i, j: (i, j)),
        out_shape=jax.ShapeDtypeStruct((m, n), BF16),
        scratch_shapes=[pltpu.VMEM((PROJ_TM, d), BF16)],
        compiler_params=_params(("parallel", "arbitrary")),
        name="mixer_qkv",
    )(x, g.reshape(1, d), w, col_gain)


A_TK = 512


def _proj_t_kernel(x_ref, g_ref, w_ref, eye_ref, o_ref, h_ref):
    @pl.when(pl.program_id(1) == 0)
    def _():
        h_ref[...] = (_rms_normalise(x_ref[...]) * g_ref[...]).astype(BF16)

    y = _dot(h_ref[...], w_ref[...]).astype(BF16)
    yt = _dot_nt(eye_ref[...], y).astype(BF16)
    for r in range(PROJ_TM // A_TK):
        o_ref[r] = yt[:, r * A_TK:(r + 1) * A_TK]


def _proj_t(x, g, w, col0, eye):
    m, d = x.shape
    n = w.shape[1] - col0
    j0 = col0 // PROJ_TN
    return pl.pallas_call(
        _proj_t_kernel,
        grid=(m // PROJ_TM, n // PROJ_TN),
        in_specs=[
            pl.BlockSpec((PROJ_TM, d), lambda i, j: (i, 0)),
            pl.BlockSpec((1, d), lambda i, j: (0, 0)),
            pl.BlockSpec((d, PROJ_TN), lambda i, j: (0, j0 + j)),
            pl.BlockSpec((PROJ_TN, PROJ_TN), lambda i, j: (0, 0)),
        ],
        out_specs=pl.BlockSpec((PROJ_TM // A_TK, PROJ_TN, A_TK), lambda i, j: (i, j, 0)),
        out_shape=jax.ShapeDtypeStruct((m // A_TK, n, A_TK), BF16),
        scratch_shapes=[pltpu.VMEM((PROJ_TM, d), BF16)],
        compiler_params=_params(("parallel", "arbitrary")),
        name="mixer_v_transposed",
    )(x, g.reshape(1, d), w, eye)


def _out_proj_kernel(a_ref, w_ref, x_ref, o_ref):
    o_ref[...] = x_ref[...] + _dot(a_ref[...], w_ref[...])


def _out_proj(a, w, x):
    m, k = a.shape
    n = w.shape[1]
    return pl.pallas_call(
        _out_proj_kernel,
        grid=(m // PROJ_TM, n // PROJ_TN),
        in_specs=[
            pl.BlockSpec((PROJ_TM, k), lambda i, j: (i, 0)),
            pl.BlockSpec((k, PROJ_TN), lambda i, j: (0, j)),
            pl.BlockSpec((PROJ_TM, PROJ_TN), lambda i, j: (i, j)),
        ],
        out_specs=pl.BlockSpec((PROJ_TM, PROJ_TN), lambda i, j: (i, j)),
        out_shape=jax.ShapeDtypeStruct((m, n), F32),
        compiler_params=_params(("parallel", "arbitrary")),
        name="mixer_out_proj",
    )(a, w, x)


A_TQ = 512


def _diff_attn_kernel(lam_ref, slope_ref, q_ref, k_ref, vt_ref, sg_ref, eye_ref, o_ref,
                      qt_ref, tbl_ref, s0_ref, s1_ref, p0_ref, p1_ref, a0_ref, a1_ref,
                      m_ref, l_ref, acc_ref, *, out_scale):
    h = pl.program_id(1)
    qi = pl.program_id(2)
    n_kv = vt_ref.shape[0]
    kv_per_q = A_TQ // A_TK
    c = slope_ref[h] * LOG2E

    @pl.when(qi == 0)
    def _():
        rows = lax.broadcasted_iota(jnp.int32, (A_TK, A_TQ), 0)
        cols = lax.broadcasted_iota(jnp.int32, (A_TK, A_TQ), 1)
        cd = c * (rows - cols).astype(F32)
        tbl_ref[0] = cd
        tbl_ref[1] = -cd
        for d in range(kv_per_q):
            tbl_ref[2 + d] = -jnp.abs(cd + c * float(d * A_TK))

    qt_ref[...] = _dot_nt(eye_ref[:A_VDIM, :A_VDIM], q_ref[...]).astype(BF16)
    m_ref[...] = jnp.full(m_ref.shape, NEG_BIG, F32)
    l_ref[...] = jnp.zeros(l_ref.shape, F32)
    acc_ref[...] = jnp.zeros(acc_ref.shape, F32)
    p1_ref[...] = jnp.zeros(p1_ref.shape, BF16)
    a1_ref[...] = jnp.ones(a1_ref.shape, F32)

    def logits(j, s_ref):
        k0 = pl.multiple_of(j * A_TK, A_TK)
        for half in range(2):
            cols_h = slice(half * HEAD_DIM, (half + 1) * HEAD_DIM)
            s_ref[half] = _dot(k_ref[pl.ds(k0, A_TK), cols_h], qt_ref[cols_h, :])

    def softmax(j, s_ref, p_ref, a_ref):
        d = j - qi * kv_per_q
        off = c * (d * A_TK).astype(F32)
        idx = jnp.where(d < 0, 0, jnp.where(d >= kv_per_q, 1, 2 + d))
        kappa = jnp.where(d < 0, off, jnp.where(d >= kv_per_q, -off, 0.0))
        for half in range(2):
            z = s_ref[half] + tbl_ref[idx]
            m_old = m_ref[half]
            m_new = jnp.maximum(m_old, jnp.max(z, axis=0, keepdims=True) + kappa)
            p = jnp.exp2(z - (m_new - kappa))
            alpha = jnp.exp2(m_old - m_new)
            l_ref[half] = alpha * l_ref[half] + jnp.sum(p, axis=0, keepdims=True)
            m_ref[half] = m_new
            a_ref[half] = alpha
            p_ref[half] = p.astype(BF16)

    def values(j, p_ref, a_ref):
        vtb = vt_ref[j]
        for half in range(2):
            acc_ref[half] = a_ref[half] * acc_ref[half] + _dot(vtb, p_ref[half])

    logits(0, s0_ref)

    def pair(jj, carry):
        j = 2 * jj
        logits(j + 1, s1_ref)
        softmax(j, s0_ref, p0_ref, a0_ref)
        values(jnp.maximum(j - 1, 0), p1_ref, a1_ref)
        logits(jnp.minimum(j + 2, n_kv - 1), s0_ref)
        softmax(j + 1, s1_ref, p1_ref, a1_ref)
        values(j, p0_ref, a0_ref)
        return carry

    lax.fori_loop(0, n_kv // 2, pair, 0)
    values(n_kv - 1, p1_ref, a1_ref)

    ot = acc_ref[0] * (1.0 / l_ref[0]) - lam_ref[0] * (acc_ref[1] * (1.0 / l_ref[1]))
    inv_rms = lax.rsqrt(jnp.mean(ot * ot, axis=0, keepdims=True) + RMS_EPS)
    ot = ot * inv_rms * (sg_ref[...] * out_scale)
    o_ref[...] = _dot_nt(eye_ref[...], ot.astype(BF16)).astype(BF16)


def _diff_attention(qk, vt, lam, subln_g, eye, batch, seq, lambda_init):
    assert A_TQ % A_TK == 0 and seq % A_TQ == 0 and eye.shape == (A_TQ, A_TQ)
    assert (seq // A_TK) % 2 == 0
    m = qk.shape[0]
    n_q = seq // A_TQ
    n_kv = seq // A_TK
    slopes = jnp.asarray(2.0 ** (-8.0 * np.arange(1, A_HEADS + 1) / A_HEADS), F32)
    kern = functools.partial(_diff_attn_kernel, out_scale=1.0 - lambda_init)
    smem = pl.BlockSpec(memory_space=pltpu.SMEM)
    return pl.pallas_call(
        kern,
        grid=(batch, A_HEADS, n_q),
        in_specs=[
            smem,
            smem,
            pl.BlockSpec((A_TQ, A_VDIM), lambda b, h, i: (b * n_q + i, h)),
            pl.BlockSpec((seq, A_VDIM), lambda b, h, i: (b, A_HEADS + h)),
            pl.BlockSpec((n_kv, A_VDIM, A_TK), lambda b, h, i: (b, h, 0)),
            pl.BlockSpec((A_VDIM, 1), lambda b, h, i: (0, 0)),
            pl.BlockSpec((A_TQ, A_TQ), lambda b, h, i: (0, 0)),
        ],
        out_specs=pl.BlockSpec((A_TQ, A_VDIM), lambda b, h, i: (b * n_q + i, h)),
        out_shape=jax.ShapeDtypeStruct((m, A_HEADS * A_VDIM), BF16),
        scratch_shapes=[
            pltpu.VMEM((A_VDIM, A_TQ), BF16),
            pltpu.VMEM((2 + A_TQ // A_TK, A_TK, A_TQ), F32),
            pltpu.VMEM((2, A_TK, A_TQ), F32),
            pltpu.VMEM((2, A_TK, A_TQ), F32),
            pltpu.VMEM((2, A_TK, A_TQ), BF16),
            pltpu.VMEM((2, A_TK, A_TQ), BF16),
            pltpu.VMEM((2, 1, A_TQ), F32),
            pltpu.VMEM((2, 1, A_TQ), F32),
            pltpu.VMEM((2, 1, A_TQ), F32),
            pltpu.VMEM((2, 1, A_TQ), F32),
            pltpu.VMEM((2, A_VDIM, A_TQ), F32),
        ],
        compiler_params=_params(("parallel", "parallel", "arbitrary")),
        name="diff_attention",
    )(lam.reshape(1), slopes, qk, qk, vt, subln_g.reshape(A_VDIM, 1), eye)


B_QROWS = 8
B_KROWS = 16
B_TQ = B_QROWS * GRID_W
B_TK = B_KROWS * GRID_W


def _na_bias_table(rel_bias, rows):
    kr = min(NA_ROWS, rows)
    n_groups = rows // B_QROWS
    n_heads, n_drow, n_dcol = rel_bias.shape
    c = np.arange(GRID_W)
    c_start = np.clip(c - NA_COLS // 2, 0, GRID_W - NA_COLS)
    kc = np.arange(GRID_W)
    col_ok = (kc[None, :] >= c_start[:, None]) & (kc[None, :] < c_start[:, None] + NA_COLS)
    dcol = kc[None, :] - c[:, None] + (NA_COLS - 1)
    select = (np.arange(n_dcol)[:, None, None] == dcol[None]) & col_ok[None]
    cols = jnp.einsum("hrd,dck->hcrk", rel_bias * LOG2E, jnp.asarray(select, F32),
                      precision=lax.Precision.HIGHEST)
    strip = jnp.where(col_ok[None, :, None, :], cols, NEG_BIG).reshape(n_heads, GRID_W, n_drow * GRID_W)
    row_blocks = []
    for g in (0, 1, n_groups - 1):
        k_row0 = int(np.clip(g * B_QROWS - kr // 2, 0, rows - B_KROWS))
        for r in range(g * B_QROWS, (g + 1) * B_QROWS):
            r_start = int(np.clip(r - kr // 2, 0, rows - kr))
            lead = r_start - k_row0
            d0 = r_start - r + (NA_ROWS - 1)
            parts = [jnp.full((n_heads, GRID_W, lead * GRID_W), NEG_BIG, F32),
                     strip[:, :, d0 * GRID_W:(d0 + kr) * GRID_W],
                     jnp.full((n_heads, GRID_W, (B_KROWS - kr - lead) * GRID_W), NEG_BIG, F32)]
            row_blocks.append(jnp.concatenate(parts, axis=-1))
    return jnp.stack(row_blocks, axis=1).reshape(n_heads, 3, B_TQ, B_TK)


def _na_kernel(q_ref, k_ref, v_ref, bias_ref, o_ref, *, rows):
    n_groups = rows // B_QROWS

    def group(g, carry):
        q0 = pl.multiple_of(g * B_TQ, B_TQ)
        k_row0 = jnp.clip(g * B_QROWS - NA_ROWS // 2, 0, rows - B_KROWS)
        k0 = pl.multiple_of(k_row0 * GRID_W, (NA_ROWS // 2) * GRID_W)
        variant = jnp.where(g == 0, 0, jnp.where(g == n_groups - 1, 2, 1))
        s = _dot_nt(q_ref[pl.ds(q0, B_TQ), :], k_ref[pl.ds(k0, B_TK), :]) + bias_ref[0, variant]
        p = jnp.exp2(s - jnp.max(s, axis=-1, keepdims=True))
        l = jnp.sum(p, axis=-1, keepdims=True)
        o = _dot(p.astype(BF16), v_ref[pl.ds(k0, B_TK), :])
        o_ref[pl.ds(q0, B_TQ), :] = (o / l).astype(BF16)
        return carry

    lax.fori_loop(0, n_groups, group, 0)


def _neighbourhood_attention(qkv, bias, batch, seq):
    m = qkv.shape[0]
    rows = seq // GRID_W
    assert rows % B_QROWS == 0 and rows >= B_KROWS and NA_ROWS <= rows
    kern = functools.partial(_na_kernel, rows=rows)
    return pl.pallas_call(
        kern,
        grid=(batch, B_HEADS),
        in_specs=[
            pl.BlockSpec((seq, HEAD_DIM), lambda b, h: (b, h)),
            pl.BlockSpec((seq, HEAD_DIM), lambda b, h: (b, B_HEADS + h)),
            pl.BlockSpec((seq, HEAD_DIM), lambda b, h: (b, 2 * B_HEADS + h)),
            pl.BlockSpec((1, 3, B_TQ, B_TK), lambda b, h: (h, 0, 0, 0)),
        ],
        out_specs=pl.BlockSpec((seq, HEAD_DIM), lambda b, h: (b, h)),
        out_shape=jax.ShapeDtypeStruct((m, B_HEADS * HEAD_DIM), BF16),
        compiler_params=_params(("parallel", "arbitrary")),
        name="neighbourhood_attention",
    )(qkv, qkv, qkv, bias)


C_TQ = 256
C_TK = C_TQ + 2 * C_WINDOW


def _swa_kernel(slope_ref, sink_ref, q_ref, k_ref, v_ref, o_ref):
    kv = pl.program_id(1)
    seq = k_ref.shape[0]
    rows = lax.broadcasted_iota(jnp.int32, (C_TQ, C_TK), 0)
    cols = lax.broadcasted_iota(jnp.int32, (C_TQ, C_TK), 1)
    delta = rows - cols

    def tile(i, carry):
        t0 = pl.multiple_of(i * C_TQ, C_TQ)
        k0 = pl.multiple_of(jnp.clip(t0 - C_WINDOW, 0, seq - C_TK), C_WINDOW)
        dist = jnp.abs(delta + (t0 - k0))
        valid = dist <= C_WINDOW
        dist_f = dist.astype(F32)
        kb = k_ref[pl.ds(k0, C_TK), :]
        vb = v_ref[pl.ds(k0, C_TK), :]
        for g in range(C_GROUP):
            head = kv * C_GROUP + g
            cols_g = slice(g * HEAD_DIM, (g + 1) * HEAD_DIM)
            s = _dot_nt(q_ref[pl.ds(t0, C_TQ), cols_g], kb)
            s = jnp.where(valid, s - (slope_ref[head] * LOG2E) * dist_f, NEG_BIG)
            sink = sink_ref[head] * LOG2E
            m = jnp.maximum(jnp.max(s, axis=-1, keepdims=True), sink)
            p = jnp.exp2(s - m)
            l = jnp.sum(p, axis=-1, keepdims=True) + jnp.exp2(sink - m)
            o = _dot(p.astype(BF16), vb)
            o_ref[pl.ds(t0, C_TQ), cols_g] = (o / l).astype(BF16)
        return carry

    lax.fori_loop(0, seq // C_TQ, tile, 0)


def _window_attention(qkv, sink, batch, seq):
    m = qkv.shape[0]
    assert seq % C_TQ == 0 and seq >= C_TK
    slopes = jnp.asarray(2.0 ** (-8.0 * np.arange(1, C_HEADS + 1) / C_HEADS), F32)
    gw = C_GROUP * HEAD_DIM
    smem = pl.BlockSpec(memory_space=pltpu.SMEM)
    return pl.pallas_call(
        _swa_kernel,
        grid=(batch, C_KV_HEADS),
        in_specs=[
            smem,
            smem,
            pl.BlockSpec((seq, gw), lambda b, kv: (b, kv)),
            pl.BlockSpec((seq, HEAD_DIM), lambda b, kv: (b, C_HEADS + kv)),
            pl.BlockSpec((seq, HEAD_DIM), lambda b, kv: (b, C_HEADS + C_KV_HEADS + kv)),
        ],
        out_specs=pl.BlockSpec((seq, gw), lambda b, kv: (b, kv)),
        out_shape=jax.ShapeDtypeStruct((m, C_HEADS * HEAD_DIM), BF16),
        compiler_params=_params(("parallel", "arbitrary")),
        name="window_attention",
    )(slopes, sink.astype(F32), qkv, qkv, qkv)


def _col_gain(q_gain, k_gain, n_q_heads, n_k_heads, n_cols):
    q_scale = HEAD_DIM ** -0.5 * LOG2E
    parts = [jnp.tile(q_gain.astype(F32) * q_scale, n_q_heads), jnp.tile(k_gain.astype(F32), n_k_heads)]
    gain = jnp.concatenate(parts)
    return jnp.pad(gain, (0, n_cols - gain.shape[0]), constant_values=1.0).reshape(1, n_cols)


def kernel(x, norm_ffn1, ffn1_w_gate, ffn1_w_up, ffn1_w_down, norm_mix, norm_ffn2, ffn2_w_gate, ffn2_w_up, ffn2_w_down, a_w_qkv, a_q_norm, a_k_norm, a_lambda_q1, a_lambda_k1, a_lambda_q2, a_lambda_k2, a_subln, a_w_o, b_w_qkv, b_q_norm, b_k_norm, b_rel_bias, b_w_o, c_w_qkv, c_q_norm, c_k_norm, c_sink, c_w_o):
    batch, seq, d = x.shape
    x = x.reshape(batch * seq, d)
    to_bf16 = lambda w: w.astype(BF16)
    ffn1 = tuple(map(to_bf16, (ffn1_w_gate, ffn1_w_up, ffn1_w_down)))
    ffn2 = tuple(map(to_bf16, (ffn2_w_gate, ffn2_w_up, ffn2_w_down)))
    a_w_qkv, a_w_o, b_w_qkv, b_w_o, c_w_qkv, c_w_o = map(
        to_bf16, (a_w_qkv, a_w_o, b_w_qkv, b_w_o, c_w_qkv, c_w_o))
    eye = jnp.eye(A_TQ, dtype=BF16)

    for i in range(DEPTH):
        x = _ffn(x, norm_ffn1[i], ffn1[0][i], ffn1[1][i], ffn1[2][i])
        kind, j = i % N_MIXERS, i // N_MIXERS
        if kind == 0:
            lambda_init = 0.8 - 0.6 * math.exp(-0.3 * i)
            lam = (jnp.exp(jnp.sum(a_lambda_q1[j].astype(F32) * a_lambda_k1[j].astype(F32)))
                   - jnp.exp(jnp.sum(a_lambda_q2[j].astype(F32) * a_lambda_k2[j].astype(F32))) + lambda_init)
            n_qk = 2 * A_HEADS * 2 * HEAD_DIM
            gain = _col_gain(a_q_norm[j], a_k_norm[j], 2 * A_HEADS, 2 * A_HEADS, n_qk)
            qk = _qkv(x, norm_mix[i], a_w_qkv[j], gain, n_qk, n=n_qk)
            vt = _proj_t(x, norm_mix[i], a_w_qkv[j], n_qk, eye)
            o = _diff_attention(qk, vt, lam, a_subln[j], eye, batch, seq, lambda_init)
            x = _out_proj(o, a_w_o[j], x)
        elif kind == 1:
            n_qk = 2 * B_HEADS * HEAD_DIM
            gain = _col_gain(b_q_norm[j], b_k_norm[j], B_HEADS, B_HEADS, b_w_qkv.shape[2])
            qkv = _qkv(x, norm_mix[i], b_w_qkv[j], gain, n_qk)
            bias = _na_bias_table(b_rel_bias[j].astype(F32), seq // GRID_W)
            o = _neighbourhood_attention(qkv, bias, batch, seq)
            x = _out_proj(o, b_w_o[j], x)
        else:
            n_qk = (C_HEADS + C_KV_HEADS) * HEAD_DIM
            gain = _col_gain(c_q_norm[j], c_k_norm[j], C_HEADS, C_KV_HEADS, c_w_qkv.shape[2])
            qkv = _qkv(x, norm_mix[i], c_w_qkv[j], gain, n_qk)
            o = _window_attention(qkv, c_sink[j], batch, seq)
            x = _out_proj(o, c_w_o[j], x)
        x = _ffn(x, norm_ffn2[i], ffn2[0][i], ffn2[1][i], ffn2[2][i])
    return x.reshape(batch, seq, d)
```

```python
import functools
import math

import jax
import jax.numpy as jnp
import numpy as np
from jax import lax
from jax.experimental import pallas as pl
from jax.experimental.pallas import tpu as pltpu

D_MODEL = 2048
DEPTH = 4
N_MIXERS = 3
HEAD_DIM = 128
D_FF = 5632
RMS_EPS = 1e-6
A_HEADS = D_MODEL // (2 * HEAD_DIM)
A_VDIM = 2 * HEAD_DIM
B_HEADS = D_MODEL // HEAD_DIM
GRID_W = 64
NA_ROWS = 8
NA_COLS = 16
C_HEADS = D_MODEL // HEAD_DIM
C_KV_HEADS = 4
C_GROUP = C_HEADS // C_KV_HEADS
C_WINDOW = 128

LOG2E = math.log2(math.e)
NEG_BIG = -1e30
BF16 = jnp.bfloat16
F32 = jnp.float32

V7X_VMEM_BYTES = 64 * 1024 * 1024
VMEM_LIMIT = 56 * 1024 * 1024
LANE = 128


def _params(semantics):
    return pltpu.CompilerParams(dimension_semantics=semantics, vmem_limit_bytes=VMEM_LIMIT)


def _dot(a, b):
    return jnp.dot(a, b, preferred_element_type=F32)


def _dot_nt(a, b):
    return lax.dot_general(a, b, (((1,), (1,)), ((), ())), preferred_element_type=F32)


def _rms_normalise(x):
    return x * lax.rsqrt(jnp.mean(x * x, axis=-1, keepdims=True) + RMS_EPS)


FFN_TM = 1024
FFN_TF = 512


def _ffn_kernel(x_ref, g_ref, wg_ref, wu_ref, wd_ref, o_ref, h_ref):
    j = pl.program_id(1)

    @pl.when(j == 0)
    def _():
        x = x_ref[...]
        h_ref[...] = (_rms_normalise(x) * g_ref[...]).astype(BF16)
        o_ref[...] = x

    h = h_ref[...]
    gate = _dot(h, wg_ref[...])
    up = _dot(h, wu_ref[...])
    act = gate * (0.5 / (1.0 + jnp.exp(-gate))) * up
    o_ref[...] += _dot(act.astype(BF16), wd_ref[...])


def _ffn(x, g, wg, wu, wd):
    m, d = x.shape
    f = wg.shape[1]
    return pl.pallas_call(
        _ffn_kernel,
        grid=(m // FFN_TM, f // FFN_TF),
        in_specs=[
            pl.BlockSpec((FFN_TM, d), lambda i, j: (i, 0)),
            pl.BlockSpec((1, d), lambda i, j: (0, 0)),
            pl.BlockSpec((d, FFN_TF), lambda i, j: (0, j)),
            pl.BlockSpec((d, FFN_TF), lambda i, j: (0, j)),
            pl.BlockSpec((FFN_TF, d), lambda i, j: (j, 0)),
        ],
        out_specs=pl.BlockSpec((FFN_TM, d), lambda i, j: (i, 0)),
        out_shape=jax.ShapeDtypeStruct((m, d), F32),
        scratch_shapes=[pltpu.VMEM((FFN_TM, d), BF16)],
        compiler_params=_params(("parallel", "arbitrary")),
        name="macaron_ffn",
    )(x, g.reshape(1, d), wg, wu, wd)


PROJ_TM = 1024
PROJ_TN = 512


def _qkv_kernel(x_ref, g_ref, w_ref, cg_ref, o_ref, h_ref, *, n_norm_blocks):
    j = pl.program_id(1)

    @pl.when(j == 0)
    def _():
        h_ref[...] = (_rms_normalise(x_ref[...]) * g_ref[...]).astype(BF16)

    y = _dot(h_ref[...], w_ref[...])

    @pl.when(j < n_norm_blocks)
    def _():
        for c in range(PROJ_TN // HEAD_DIM):
            cols = slice(c * HEAD_DIM, (c + 1) * HEAD_DIM)
            o_ref[:, cols] = (_rms_normalise(y[:, cols]) * cg_ref[:, cols]).astype(BF16)

    @pl.when(j >= n_norm_blocks)
    def _():
        o_ref[...] = y.astype(BF16)


def _qkv(x, g, w, col_gain, n_norm_cols, n=None):
    m, d = x.shape
    n = w.shape[1] if n is None else n
    kern = functools.partial(_qkv_kernel, n_norm_blocks=n_norm_cols // PROJ_TN)
    return pl.pallas_call(
        kern,
        grid=(m // PROJ_TM, n // PROJ_TN),
        in_specs=[
            pl.BlockSpec((PROJ_TM, d), lambda i, j: (i, 0)),
            pl.BlockSpec((1, d), lambda i, j: (0, 0)),
            pl.BlockSpec((d, PROJ_TN), lambda i, j: (0, j)),
            pl.BlockSpec((1, PROJ_TN), lambda i, j: (0, j)),
        ],
        out_specs=pl.BlockSpec((PROJ_TM, PROJ_TN), lambda i, j: (i, j)),
        out_shape=jax.ShapeDtypeStruct((m, n), BF16),
        scratch_shapes=[pltpu.VMEM((PROJ_TM, d), BF16)],
        compiler_params=_params(("parallel", "arbitrary")),
        name="mixer_qkv",
    )(x, g.reshape(1, d), w, col_gain)


A_TK = 512


def _proj_t_kernel(x_ref, g_ref, w_ref, eye_ref, o_ref, h_ref):
    @pl.when(pl.program_id(1) == 0)
    def _():
        h_ref[...] = (_rms_normalise(x_ref[...]) * g_ref[...]).astype(BF16)

    y = _dot(h_ref[...], w_ref[...]).astype(BF16)
    yt = _dot_nt(eye_ref[...], y).astype(BF16)
    for r in range(PROJ_TM // A_TK):
        o_ref[r] = yt[:, r * A_TK:(r + 1) * A_TK]


def _proj_t(x, g, w, col0, eye):
    m, d = x.shape
    n = w.shape[1] - col0
    j0 = col0 // PROJ_TN
    return pl.pallas_call(
        _proj_t_kernel,
        grid=(m // PROJ_TM, n // PROJ_TN),
        in_specs=[
            pl.BlockSpec((PROJ_TM, d), lambda i, j: (i, 0)),
            pl.BlockSpec((1, d), lambda i, j: (0, 0)),
            pl.BlockSpec((d, PROJ_TN), lambda i, j: (0, j0 + j)),
            pl.BlockSpec((PROJ_TN, PROJ_TN), lambda i, j: (0, 0)),
        ],
        out_specs=pl.BlockSpec((PROJ_TM // A_TK, PROJ_TN, A_TK), lambda i, j: (i, j, 0)),
        out_shape=jax.ShapeDtypeStruct((m // A_TK, n, A_TK), BF16),
        scratch_shapes=[pltpu.VMEM((PROJ_TM, d), BF16)],
        compiler_params=_params(("parallel", "arbitrary")),
        name="mixer_v_transposed",
    )(x, g.reshape(1, d), w, eye)


def _out_proj_kernel(a_ref, w_ref, x_ref, o_ref):
    o_ref[...] = x_ref[...] + _dot(a_ref[...], w_ref[...])


def _out_proj(a, w, x):
    m, k = a.shape
    n = w.shape[1]
    return pl.pallas_call(
        _out_proj_kernel,
        grid=(m // PROJ_TM, n // PROJ_TN),
        in_specs=[
            pl.BlockSpec((PROJ_TM, k), lambda i, j: (i, 0)),
            pl.BlockSpec((k, PROJ_TN), lambda i, j: (0, j)),
            pl.BlockSpec((PROJ_TM, PROJ_TN), lambda i, j: (i, j)),
        ],
        out_specs=pl.BlockSpec((PROJ_TM, PROJ_TN), lambda i, j: (i, j)),
        out_shape=jax.ShapeDtypeStruct((m, n), F32),
        compiler_params=_params(("parallel", "arbitrary")),
        name="mixer_out_proj",
    )(a, w, x)


A_TQ = 512
A_MAX_FIXED_SHIFT = 40.0


def _diff_attn_kernel(scal_ref, slope_ref, q_ref, k_ref, vt_ref, sg_ref, eye_ref, o_ref,
                      qt_ref, tbl_ref, s0_ref, s1_ref, x0_ref, x1_ref, p0_ref, p1_ref, a0_ref, a1_ref,
                      m_ref, l_ref, acc_ref, *, out_scale):
    h = pl.program_id(1)
    qi = pl.program_id(2)
    n_kv = vt_ref.shape[0]
    kv_per_q = A_TQ // A_TK
    c = slope_ref[h] * LOG2E

    @pl.when(qi == 0)
    def _():
        rows = lax.broadcasted_iota(jnp.int32, (A_TK, A_TQ), 0)
        cols = lax.broadcasted_iota(jnp.int32, (A_TK, A_TQ), 1)
        cd = c * (rows - cols).astype(F32)
        tbl_ref[0] = cd
        tbl_ref[1] = -cd
        for d in range(kv_per_q):
            tbl_ref[2 + d] = -jnp.abs(cd + c * float(d * A_TK))

    qt_ref[...] = _dot_nt(eye_ref[:A_VDIM, :A_VDIM], q_ref[...]).astype(BF16)
    m_ref[...] = jnp.full(m_ref.shape, NEG_BIG, F32)
    l_ref[...] = jnp.zeros(l_ref.shape, F32)
    acc_ref[...] = jnp.zeros(acc_ref.shape, F32)
    p1_ref[...] = jnp.zeros(p1_ref.shape, BF16)
    a1_ref[...] = jnp.ones(a1_ref.shape, F32)

    def bias_of(j):
        d = j - qi * kv_per_q
        off = c * (d * A_TK).astype(F32)
        idx = jnp.where(d < 0, 0, jnp.where(d >= kv_per_q, 1, 2 + d))
        kappa = jnp.where(d < 0, off, jnp.where(d >= kv_per_q, -off, 0.0))
        return idx, kappa

    def biased_logits(j, half, idx):
        k0 = pl.multiple_of(j * A_TK, A_TK)
        cols_h = slice(half * HEAD_DIM, (half + 1) * HEAD_DIM)
        return _dot(k_ref[pl.ds(k0, A_TK), cols_h], qt_ref[cols_h, :]) + tbl_ref[idx]

    bound = scal_ref[1]
    fixed_shift_ok = bound <= A_MAX_FIXED_SHIFT

    @pl.when(fixed_shift_ok)
    def _():
        def probabilities(j, p_ref, live):
            idx, kappa = bias_of(j)
            shift = kappa - bound
            for half in range(2):
                p = jnp.exp2(biased_logits(j, half, idx) + shift)
                l_ref[half] += live * jnp.sum(p, axis=0, keepdims=True)
                p_ref[half] = p.astype(BF16)

        def values(j, p_ref):
            vtb = vt_ref[j]
            for half in range(2):
                acc_ref[half] += _dot(vtb, p_ref[half])

        probabilities(0, p0_ref, 1.0)

        def pair(jj, carry):
            j = 2 * jj
            probabilities(j + 1, p1_ref, 1.0)
            values(j, p0_ref)
            nxt = j + 2
            probabilities(jnp.minimum(nxt, n_kv - 1), p0_ref, (nxt < n_kv).astype(F32))
            values(j + 1, p1_ref)
            return carry

        lax.fori_loop(0, n_kv // 2, pair, 0)

    @pl.when(jnp.logical_not(fixed_shift_ok))
    def _():
        def logits(j, z_ref, zmax_ref):
            idx, _ = bias_of(j)
            for half in range(2):
                z = biased_logits(j, half, idx)
                z_ref[half] = z
                zmax_ref[half] = jnp.max(z, axis=0, keepdims=True)

        def softmax(j, z_ref, zmax_ref, p_ref, a_ref):
            _, kappa = bias_of(j)
            for half in range(2):
                m_old = m_ref[half]
                m_new = jnp.maximum(m_old, zmax_ref[half] + kappa)
                p = jnp.exp2(z_ref[half] - (m_new - kappa))
                alpha = jnp.exp2(m_old - m_new)
                l_ref[half] = alpha * l_ref[half] + jnp.sum(p, axis=0, keepdims=True)
                m_ref[half] = m_new
                a_ref[half] = alpha
                p_ref[half] = p.astype(BF16)

        def values(j, p_ref, a_ref):
            vtb = vt_ref[j]
            for half in range(2):
                acc_ref[half] = a_ref[half] * acc_ref[half] + _dot(vtb, p_ref[half])

        logits(0, s0_ref, x0_ref)

        def pair(jj, carry):
            j = 2 * jj
            logits(j + 1, s1_ref, x1_ref)
            softmax(j, s0_ref, x0_ref, p0_ref, a0_ref)
            values(jnp.maximum(j - 1, 0), p1_ref, a1_ref)
            logits(jnp.minimum(j + 2, n_kv - 1), s0_ref, x0_ref)
            softmax(j + 1, s1_ref, x1_ref, p1_ref, a1_ref)
            values(j, p0_ref, a0_ref)
            return carry

        lax.fori_loop(0, n_kv // 2, pair, 0)
        values(n_kv - 1, p1_ref, a1_ref)

    ot = acc_ref[0] * (1.0 / l_ref[0]) - scal_ref[0] * (acc_ref[1] * (1.0 / l_ref[1]))
    inv_rms = lax.rsqrt(jnp.mean(ot * ot, axis=0, keepdims=True) + RMS_EPS)
    ot = ot * inv_rms * (sg_ref[...] * out_scale)
    o_ref[...] = _dot_nt(eye_ref[...], ot.astype(BF16)).astype(BF16)


def _logit_bound(q_gain, k_gain):
    q_scale = HEAD_DIM ** -0.5 * LOG2E
    return (HEAD_DIM * q_scale * 1.02) * jnp.max(jnp.abs(q_gain.astype(F32))) * jnp.max(jnp.abs(k_gain.astype(F32)))


def _diff_attention(qk, vt, lam, bound, subln_g, eye, batch, seq, lambda_init):
    assert A_TQ % A_TK == 0 and seq % A_TQ == 0 and eye.shape == (A_TQ, A_TQ)
    assert (seq // A_TK) % 2 == 0
    m = qk.shape[0]
    n_q = seq // A_TQ
    n_kv = seq // A_TK
    slopes = jnp.asarray(2.0 ** (-8.0 * np.arange(1, A_HEADS + 1) / A_HEADS), F32)
    kern = functools.partial(_diff_attn_kernel, out_scale=1.0 - lambda_init)
    smem = pl.BlockSpec(memory_space=pltpu.SMEM)
    return pl.pallas_call(
        kern,
        grid=(batch, A_HEADS, n_q),
        in_specs=[
            smem,
            smem,
            pl.BlockSpec((A_TQ, A_VDIM), lambda b, h, i: (b * n_q + i, h)),
            pl.BlockSpec((seq, A_VDIM), lambda b, h, i: (b, A_HEADS + h)),
            pl.BlockSpec((n_kv, A_VDIM, A_TK), lambda b, h, i: (b, h, 0)),
            pl.BlockSpec((A_VDIM, 1), lambda b, h, i: (0, 0)),
            pl.BlockSpec((A_TQ, A_TQ), lambda b, h, i: (0, 0)),
        ],
        out_specs=pl.BlockSpec((A_TQ, A_VDIM), lambda b, h, i: (b * n_q + i, h)),
        out_shape=jax.ShapeDtypeStruct((m, A_HEADS * A_VDIM), BF16),
        scratch_shapes=[
            pltpu.VMEM((A_VDIM, A_TQ), BF16),
            pltpu.VMEM((2 + A_TQ // A_TK, A_TK, A_TQ), F32),
            pltpu.VMEM((2, A_TK, A_TQ), F32),
            pltpu.VMEM((2, A_TK, A_TQ), F32),
            pltpu.VMEM((2, 1, A_TQ), F32),
            pltpu.VMEM((2, 1, A_TQ), F32),
            pltpu.VMEM((2, A_TK, A_TQ), BF16),
            pltpu.VMEM((2, A_TK, A_TQ), BF16),
            pltpu.VMEM((2, 1, A_TQ), F32),
            pltpu.VMEM((2, 1, A_TQ), F32),
            pltpu.VMEM((2, 1, A_TQ), F32),
            pltpu.VMEM((2, 1, A_TQ), F32),
            pltpu.VMEM((2, A_VDIM, A_TQ), F32),
        ],
        compiler_params=_params(("parallel", "parallel", "arbitrary")),
        name="diff_attention",
    )(jnp.stack([lam, bound]).astype(F32), slopes, qk, qk, vt, subln_g.reshape(A_VDIM, 1), eye)


B_QROWS = 8
B_KROWS = 16
B_TQ = B_QROWS * GRID_W
B_TK = B_KROWS * GRID_W


def _na_bias_table(rel_bias, rows):
    kr = min(NA_ROWS, rows)
    n_groups = rows // B_QROWS
    n_heads, n_drow, n_dcol = rel_bias.shape
    c = np.arange(GRID_W)
    c_start = np.clip(c - NA_COLS // 2, 0, GRID_W - NA_COLS)
    kc = np.arange(GRID_W)
    col_ok = (kc[None, :] >= c_start[:, None]) & (kc[None, :] < c_start[:, None] + NA_COLS)
    dcol = kc[None, :] - c[:, None] + (NA_COLS - 1)
    select = (np.arange(n_dcol)[:, None, None] == dcol[None]) & col_ok[None]
    cols = jnp.einsum("hrd,dck->hcrk", rel_bias * LOG2E, jnp.asarray(select, F32),
                      precision=lax.Precision.HIGHEST)
    strip = jnp.where(col_ok[None, :, None, :], cols, NEG_BIG).reshape(n_heads, GRID_W, n_drow * GRID_W)
    row_blocks = []
    for g in (0, 1, n_groups - 1):
        k_row0 = int(np.clip(g * B_QROWS - kr // 2, 0, rows - B_KROWS))
        for r in range(g * B_QROWS, (g + 1) * B_QROWS):
            r_start = int(np.clip(r - kr // 2, 0, rows - kr))
            lead = r_start - k_row0
            d0 = r_start - r + (NA_ROWS - 1)
            parts = [jnp.full((n_heads, GRID_W, lead * GRID_W), NEG_BIG, F32),
                     strip[:, :, d0 * GRID_W:(d0 + kr) * GRID_W],
                     jnp.full((n_heads, GRID_W, (B_KROWS - kr - lead) * GRID_W), NEG_BIG, F32)]
            row_blocks.append(jnp.concatenate(parts, axis=-1))
    return jnp.stack(row_blocks, axis=1).reshape(n_heads, 3, B_TQ, B_TK)


def _na_kernel(q_ref, k_ref, v_ref, bias_ref, o_ref, *, rows):
    n_groups = rows // B_QROWS

    def group(g, carry):
        q0 = pl.multiple_of(g * B_TQ, B_TQ)
        k_row0 = jnp.clip(g * B_QROWS - NA_ROWS // 2, 0, rows - B_KROWS)
        k0 = pl.multiple_of(k_row0 * GRID_W, (NA_ROWS // 2) * GRID_W)
        variant = jnp.where(g == 0, 0, jnp.where(g == n_groups - 1, 2, 1))
        s = _dot_nt(q_ref[pl.ds(q0, B_TQ), :], k_ref[pl.ds(k0, B_TK), :]) + bias_ref[0, variant]
        p = jnp.exp2(s - jnp.max(s, axis=-1, keepdims=True))
        l = jnp.sum(p, axis=-1, keepdims=True)
        o = _dot(p.astype(BF16), v_ref[pl.ds(k0, B_TK), :])
        o_ref[pl.ds(q0, B_TQ), :] = (o / l).astype(BF16)
        return carry

    lax.fori_loop(0, n_groups, group, 0)


def _neighbourhood_attention(qkv, bias, batch, seq):
    m = qkv.shape[0]
    rows = seq // GRID_W
    assert rows % B_QROWS == 0 and rows >= B_KROWS and NA_ROWS <= rows
    kern = functools.partial(_na_kernel, rows=rows)
    return pl.pallas_call(
        kern,
        grid=(batch, B_HEADS),
        in_specs=[
            pl.BlockSpec((seq, HEAD_DIM), lambda b, h: (b, h)),
            pl.BlockSpec((seq, HEAD_DIM), lambda b, h: (b, B_HEADS + h)),
            pl.BlockSpec((seq, HEAD_DIM), lambda b, h: (b, 2 * B_HEADS + h)),
            pl.BlockSpec((1, 3, B_TQ, B_TK), lambda b, h: (h, 0, 0, 0)),
        ],
        out_specs=pl.BlockSpec((seq, HEAD_DIM), lambda b, h: (b, h)),
        out_shape=jax.ShapeDtypeStruct((m, B_HEADS * HEAD_DIM), BF16),
        compiler_params=_params(("parallel", "arbitrary")),
        name="neighbourhood_attention",
    )(qkv, qkv, qkv, bias)


C_TQ = 256
C_TK = C_TQ + 2 * C_WINDOW


def _swa_kernel(slope_ref, sink_ref, q_ref, k_ref, v_ref, o_ref):
    kv = pl.program_id(1)
    seq = k_ref.shape[0]
    rows = lax.broadcasted_iota(jnp.int32, (C_TQ, C_TK), 0)
    cols = lax.broadcasted_iota(jnp.int32, (C_TQ, C_TK), 1)
    delta = rows - cols

    def tile(i, carry):
        t0 = pl.multiple_of(i * C_TQ, C_TQ)
        k0 = pl.multiple_of(jnp.clip(t0 - C_WINDOW, 0, seq - C_TK), C_WINDOW)
        dist = jnp.abs(delta + (t0 - k0))
        valid = dist <= C_WINDOW
        dist_f = dist.astype(F32)
        kb = k_ref[pl.ds(k0, C_TK), :]
        vb = v_ref[pl.ds(k0, C_TK), :]
        for g in range(C_GROUP):
            head = kv * C_GROUP + g
            cols_g = slice(g * HEAD_DIM, (g + 1) * HEAD_DIM)
            s = _dot_nt(q_ref[pl.ds(t0, C_TQ), cols_g], kb)
            s = jnp.where(valid, s - (slope_ref[head] * LOG2E) * dist_f, NEG_BIG)
            sink = sink_ref[head] * LOG2E
            m = jnp.maximum(jnp.max(s, axis=-1, keepdims=True), sink)
            p = jnp.exp2(s - m)
            l = jnp.sum(p, axis=-1, keepdims=True) + jnp.exp2(sink - m)
            o = _dot(p.astype(BF16), vb)
            o_ref[pl.ds(t0, C_TQ), cols_g] = (o / l).astype(BF16)
        return carry

    lax.fori_loop(0, seq // C_TQ, tile, 0)


def _window_attention(qkv, sink, batch, seq):
    m = qkv.shape[0]
    assert seq % C_TQ == 0 and seq >= C_TK
    slopes = jnp.asarray(2.0 ** (-8.0 * np.arange(1, C_HEADS + 1) / C_HEADS), F32)
    gw = C_GROUP * HEAD_DIM
    smem = pl.BlockSpec(memory_space=pltpu.SMEM)
    return pl.pallas_call(
        _swa_kernel,
        grid=(batch, C_KV_HEADS),
        in_specs=[
            smem,
            smem,
            pl.BlockSpec((seq, gw), lambda b, kv: (b, kv)),
            pl.BlockSpec((seq, HEAD_DIM), lambda b, kv: (b, C_HEADS + kv)),
            pl.BlockSpec((seq, HEAD_DIM), lambda b, kv: (b, C_HEADS + C_KV_HEADS + kv)),
        ],
        out_specs=pl.BlockSpec((seq, gw), lambda b, kv: (b, kv)),
        out_shape=jax.ShapeDtypeStruct((m, C_HEADS * HEAD_DIM), BF16),
        compiler_params=_params(("parallel", "arbitrary")),
        name="window_attention",
    )(slopes, sink.astype(F32), qkv, qkv, qkv)


def _col_gain(q_gain, k_gain, n_q_heads, n_k_heads, n_cols):
    q_scale = HEAD_DIM ** -0.5 * LOG2E
    parts = [jnp.tile(q_gain.astype(F32) * q_scale, n_q_heads), jnp.tile(k_gain.astype(F32), n_k_heads)]
    gain = jnp.concatenate(parts)
    return jnp.pad(gain, (0, n_cols - gain.shape[0]), constant_values=1.0).reshape(1, n_cols)


def kernel(x, norm_ffn1, ffn1_w_gate, ffn1_w_up, ffn1_w_down, norm_mix, norm_ffn2, ffn2_w_gate, ffn2_w_up, ffn2_w_down, a_w_qkv, a_q_norm, a_k_norm, a_lambda_q1, a_lambda_k1, a_lambda_q2, a_lambda_k2, a_subln, a_w_o, b_w_qkv, b_q_norm, b_k_norm, b_rel_bias, b_w_o, c_w_qkv, c_q_norm, c_k_norm, c_sink, c_w_o):
    batch, seq, d = x.shape
    x = x.reshape(batch * seq, d)
    to_bf16 = lambda w: w.astype(BF16)
    ffn1 = tuple(map(to_bf16, (ffn1_w_gate, ffn1_w_up, ffn1_w_down)))
    ffn2 = tuple(map(to_bf16, (ffn2_w_gate, ffn2_w_up, ffn2_w_down)))
    a_w_qkv, a_w_o, b_w_qkv, b_w_o, c_w_qkv, c_w_o = map(
        to_bf16, (a_w_qkv, a_w_o, b_w_qkv, b_w_o, c_w_qkv, c_w_o))
    eye = jnp.eye(A_TQ, dtype=BF16)

    for i in range(DEPTH):
        x = _ffn(x, norm_ffn1[i], ffn1[0][i], ffn1[1][i], ffn1[2][i])
        kind, j = i % N_MIXERS, i // N_MIXERS
        if kind == 0:
            lambda_init = 0.8 - 0.6 * math.exp(-0.3 * i)
            lam = (jnp.exp(jnp.sum(a_lambda_q1[j].astype(F32) * a_lambda_k1[j].astype(F32)))
                   - jnp.exp(jnp.sum(a_lambda_q2[j].astype(F32) * a_lambda_k2[j].astype(F32))) + lambda_init)
            n_qk = 2 * A_HEADS * 2 * HEAD_DIM
            gain = _col_gain(a_q_norm[j], a_k_norm[j], 2 * A_HEADS, 2 * A_HEADS, n_qk)
            qk = _qkv(x, norm_mix[i], a_w_qkv[j], gain, n_qk, n=n_qk)
            vt = _proj_t(x, norm_mix[i], a_w_qkv[j], n_qk, eye)
            bound = _logit_bound(a_q_norm[j], a_k_norm[j])
            o = _diff_attention(qk, vt, lam, bound, a_subln[j], eye, batch, seq, lambda_init)
            x = _out_proj(o, a_w_o[j], x)
        elif kind == 1:
            n_qk = 2 * B_HEADS * HEAD_DIM
            gain = _col_gain(b_q_norm[j], b_k_norm[j], B_HEADS, B_HEADS, b_w_qkv.shape[2])
            qkv = _qkv(x, norm_mix[i], b_w_qkv[j], gain, n_qk)
            bias = _na_bias_table(b_rel_bias[j].astype(F32), seq // GRID_W)
            o = _neighbourhood_attention(qkv, bias, batch, seq)
            x = _out_proj(o, b_w_o[j], x)
        else:
            n_qk = (C_HEADS + C_KV_HEADS) * HEAD_DIM
            gain = _col_gain(c_q_norm[j], c_k_norm[j], C_HEADS, C_KV_HEADS, c_w_qkv.shape[2])
            qkv = _qkv(x, norm_mix[i], c_w_qkv[j], gain, n_qk)
            o = _window_attention(qkv, c_sink[j], batch, seq)
            x = _out_proj(o, c_w_o[j], x)
        x = _ffn(x, norm_ffn2[i], ffn2[0][i], ffn2[1][i], ffn2[2][i])
    return x.reshape(batch, seq, d)
```

```python
import functools
import math

import jax
import jax.numpy as jnp
import numpy as np
from jax import lax
from jax.experimental import pallas as pl
from jax.experimental.pallas import tpu as pltpu

D_MODEL = 2048
DEPTH = 4
N_MIXERS = 3
HEAD_DIM = 128
D_FF = 5632
RMS_EPS = 1e-6
A_HEADS = D_MODEL // (2 * HEAD_DIM)
A_VDIM = 2 * HEAD_DIM
B_HEADS = D_MODEL // HEAD_DIM
GRID_W = 64
NA_ROWS = 8
NA_COLS = 16
C_HEADS = D_MODEL // HEAD_DIM
C_KV_HEADS = 4
C_GROUP = C_HEADS // C_KV_HEADS
C_WINDOW = 128

LOG2E = math.log2(math.e)
NEG_BIG = -1e30
BF16 = jnp.bfloat16
F32 = jnp.float32

V7X_VMEM_BYTES = 64 * 1024 * 1024
VMEM_LIMIT = 56 * 1024 * 1024
LANE = 128


def _params(semantics):
    return pltpu.CompilerParams(dimension_semantics=semantics, vmem_limit_bytes=VMEM_LIMIT)


def _dot(a, b):
    return jnp.dot(a, b, preferred_element_type=F32)


def _dot_nt(a, b):
    return lax.dot_general(a, b, (((1,), (1,)), ((), ())), preferred_element_type=F32)


def _rms_normalise(x):
    return x * lax.rsqrt(jnp.mean(x * x, axis=-1, keepdims=True) + RMS_EPS)


FFN_TM = 1024
FFN_TF = 512


def _ffn_kernel(x_ref, g_ref, wg_ref, wu_ref, wd_ref, o_ref, h_ref):
    j = pl.program_id(1)

    @pl.when(j == 0)
    def _():
        x = x_ref[...]
        h_ref[...] = (_rms_normalise(x) * g_ref[...]).astype(BF16)
        o_ref[...] = x

    h = h_ref[...]
    gate = _dot(h, wg_ref[...])
    up = _dot(h, wu_ref[...])
    act = gate * (0.5 / (1.0 + jnp.exp(-gate))) * up
    o_ref[...] += _dot(act.astype(BF16), wd_ref[...])


def _ffn(x, g, wg, wu, wd, layer):
    m, d = x.shape
    f = wg.shape[2]
    return pl.pallas_call(
        _ffn_kernel,
        grid=(m // FFN_TM, f // FFN_TF),
        in_specs=[
            pl.BlockSpec((FFN_TM, d), lambda i, j: (i, 0)),
            pl.BlockSpec((1, d), lambda i, j: (0, 0)),
            pl.BlockSpec((None, d, FFN_TF), lambda i, j: (layer, 0, j)),
            pl.BlockSpec((None, d, FFN_TF), lambda i, j: (layer, 0, j)),
            pl.BlockSpec((None, FFN_TF, d), lambda i, j: (layer, j, 0)),
        ],
        out_specs=pl.BlockSpec((FFN_TM, d), lambda i, j: (i, 0)),
        out_shape=jax.ShapeDtypeStruct((m, d), F32),
        scratch_shapes=[pltpu.VMEM((FFN_TM, d), BF16)],
        compiler_params=_params(("parallel", "arbitrary")),
        name="macaron_ffn",
    )(x, g.reshape(1, d), wg, wu, wd)


PROJ_TM = 1024
PROJ_TN = 512


def _qkv_kernel(x_ref, g_ref, w_ref, cg_ref, hm_ref, o_ref, h_ref, *, n_norm_blocks):
    j = pl.program_id(1)

    @pl.when(j == 0)
    def _():
        h_ref[...] = (_rms_normalise(x_ref[...]) * g_ref[...]).astype(BF16)

    y = _dot(h_ref[...], w_ref[...])

    @pl.when(j < n_norm_blocks)
    def _():
        ms = _dot((y * y).astype(BF16), hm_ref[...])
        o_ref[...] = (y * lax.rsqrt(ms + RMS_EPS) * cg_ref[...]).astype(BF16)

    @pl.when(j >= n_norm_blocks)
    def _():
        o_ref[...] = y.astype(BF16)


def _head_mean_matrix():
    head = np.arange(PROJ_TN) // HEAD_DIM
    return jnp.asarray((head[:, None] == head[None, :]) / HEAD_DIM, BF16)


def _qkv(x, g, w, layer, col_gain, n_norm_cols, n=None):
    m, d = x.shape
    n = w.shape[2] if n is None else n
    kern = functools.partial(_qkv_kernel, n_norm_blocks=n_norm_cols // PROJ_TN)
    return pl.pallas_call(
        kern,
        grid=(m // PROJ_TM, n // PROJ_TN),
        in_specs=[
            pl.BlockSpec((PROJ_TM, d), lambda i, j: (i, 0)),
            pl.BlockSpec((1, d), lambda i, j: (0, 0)),
            pl.BlockSpec((None, d, PROJ_TN), lambda i, j: (layer, 0, j)),
            pl.BlockSpec((1, PROJ_TN), lambda i, j: (0, j)),
            pl.BlockSpec((PROJ_TN, PROJ_TN), lambda i, j: (0, 0)),
        ],
        out_specs=pl.BlockSpec((PROJ_TM, PROJ_TN), lambda i, j: (i, j)),
        out_shape=jax.ShapeDtypeStruct((m, n), BF16),
        scratch_shapes=[pltpu.VMEM((PROJ_TM, d), BF16)],
        compiler_params=_params(("parallel", "arbitrary")),
        name="mixer_qkv",
    )(x, g.reshape(1, d), w, col_gain, _head_mean_matrix())


A_TK = 512


def _proj_t_kernel(x_ref, g_ref, w_ref, eye_ref, o_ref, h_ref):
    @pl.when(pl.program_id(1) == 0)
    def _():
        h_ref[...] = (_rms_normalise(x_ref[...]) * g_ref[...]).astype(BF16)

    y = _dot(h_ref[...], w_ref[...]).astype(BF16)
    yt = _dot_nt(eye_ref[...], y).astype(BF16)
    for r in range(PROJ_TM // A_TK):
        o_ref[r] = yt[:, r * A_TK:(r + 1) * A_TK]


def _proj_t(x, g, w, layer, col0, eye):
    m, d = x.shape
    n = w.shape[2] - col0
    j0 = col0 // PROJ_TN
    return pl.pallas_call(
        _proj_t_kernel,
        grid=(m // PROJ_TM, n // PROJ_TN),
        in_specs=[
            pl.BlockSpec((PROJ_TM, d), lambda i, j: (i, 0)),
            pl.BlockSpec((1, d), lambda i, j: (0, 0)),
            pl.BlockSpec((None, d, PROJ_TN), lambda i, j: (layer, 0, j0 + j)),
            pl.BlockSpec((PROJ_TN, PROJ_TN), lambda i, j: (0, 0)),
        ],
        out_specs=pl.BlockSpec((PROJ_TM // A_TK, PROJ_TN, A_TK), lambda i, j: (i, j, 0)),
        out_shape=jax.ShapeDtypeStruct((m // A_TK, n, A_TK), BF16),
        scratch_shapes=[pltpu.VMEM((PROJ_TM, d), BF16)],
        compiler_params=_params(("parallel", "arbitrary")),
        name="mixer_v_transposed",
    )(x, g.reshape(1, d), w, eye)


def _out_proj_kernel(a_ref, w_ref, x_ref, o_ref):
    o_ref[...] = x_ref[...] + _dot(a_ref[...], w_ref[...])


def _out_proj(a, w, layer, x):
    m, k = a.shape
    n = w.shape[2]
    return pl.pallas_call(
        _out_proj_kernel,
        grid=(m // PROJ_TM, n // PROJ_TN),
        in_specs=[
            pl.BlockSpec((PROJ_TM, k), lambda i, j: (i, 0)),
            pl.BlockSpec((None, k, PROJ_TN), lambda i, j: (layer, 0, j)),
            pl.BlockSpec((PROJ_TM, PROJ_TN), lambda i, j: (i, j)),
        ],
        out_specs=pl.BlockSpec((PROJ_TM, PROJ_TN), lambda i, j: (i, j)),
        out_shape=jax.ShapeDtypeStruct((m, n), F32),
        compiler_params=_params(("parallel", "arbitrary")),
        name="mixer_out_proj",
    )(a, w, x)


A_TQ = 512
A_MAX_FIXED_SHIFT = 40.0


def _diff_attn_kernel(scal_ref, slope_ref, q_ref, k_ref, vt_ref, sg_ref, eye_ref, o_ref,
                      qt_ref, tbl_ref, s0_ref, s1_ref, x0_ref, x1_ref, p0_ref, p1_ref, a0_ref, a1_ref,
                      m_ref, l_ref, acc_ref, *, out_scale):
    h = pl.program_id(1)
    qi = pl.program_id(2)
    n_kv = vt_ref.shape[0]
    kv_per_q = A_TQ // A_TK
    c = slope_ref[h] * LOG2E

    @pl.when(qi == 0)
    def _():
        rows = lax.broadcasted_iota(jnp.int32, (A_TK, A_TQ), 0)
        cols = lax.broadcasted_iota(jnp.int32, (A_TK, A_TQ), 1)
        cd = c * (rows - cols).astype(F32)
        tbl_ref[0] = cd
        tbl_ref[1] = -cd
        for d in range(kv_per_q):
            tbl_ref[2 + d] = -jnp.abs(cd + c * float(d * A_TK))

    qt_ref[...] = _dot_nt(eye_ref[:A_VDIM, :A_VDIM], q_ref[...]).astype(BF16)
    m_ref[...] = jnp.full(m_ref.shape, NEG_BIG, F32)
    l_ref[...] = jnp.zeros(l_ref.shape, F32)
    acc_ref[...] = jnp.zeros(acc_ref.shape, F32)
    p1_ref[...] = jnp.zeros(p1_ref.shape, BF16)
    a1_ref[...] = jnp.ones(a1_ref.shape, F32)

    def bias_of(j):
        d = j - qi * kv_per_q
        off = c * (d * A_TK).astype(F32)
        idx = jnp.where(d < 0, 0, jnp.where(d >= kv_per_q, 1, 2 + d))
        kappa = jnp.where(d < 0, off, jnp.where(d >= kv_per_q, -off, 0.0))
        return idx, kappa

    def biased_logits(j, half, idx):
        k0 = pl.multiple_of(j * A_TK, A_TK)
        cols_h = slice(half * HEAD_DIM, (half + 1) * HEAD_DIM)
        return _dot(k_ref[pl.ds(k0, A_TK), cols_h], qt_ref[cols_h, :]) + tbl_ref[idx]

    bound = scal_ref[1]
    fixed_shift_ok = bound <= A_MAX_FIXED_SHIFT

    @pl.when(fixed_shift_ok)
    def _():
        def probabilities(j, p_ref, live):
            idx, kappa = bias_of(j)
            shift = kappa - bound
            for half in range(2):
                p = jnp.exp2(biased_logits(j, half, idx) + shift)
                l_ref[half] += live * jnp.sum(p, axis=0, keepdims=True)
                p_ref[half] = p.astype(BF16)

        def values(j, p_ref):
            vtb = vt_ref[j]
            for half in range(2):
                acc_ref[half] += _dot(vtb, p_ref[half])

        probabilities(0, p0_ref, 1.0)

        def pair(jj, carry):
            j = 2 * jj
            probabilities(j + 1, p1_ref, 1.0)
            values(j, p0_ref)
            nxt = j + 2
            probabilities(jnp.minimum(nxt, n_kv - 1), p0_ref, (nxt < n_kv).astype(F32))
            values(j + 1, p1_ref)
            return carry

        lax.fori_loop(0, n_kv // 2, pair, 0)

    @pl.when(jnp.logical_not(fixed_shift_ok))
    def _():
        def logits(j, z_ref, zmax_ref):
            idx, _ = bias_of(j)
            for half in range(2):
                z = biased_logits(j, half, idx)
                z_ref[half] = z
                zmax_ref[half] = jnp.max(z, axis=0, keepdims=True)

        def softmax(j, z_ref, zmax_ref, p_ref, a_ref):
            _, kappa = bias_of(j)
            for half in range(2):
                m_old = m_ref[half]
                m_new = jnp.maximum(m_old, zmax_ref[half] + kappa)
                p = jnp.exp2(z_ref[half] - (m_new - kappa))
                alpha = jnp.exp2(m_old - m_new)
                l_ref[half] = alpha * l_ref[half] + jnp.sum(p, axis=0, keepdims=True)
                m_ref[half] = m_new
                a_ref[half] = alpha
                p_ref[half] = p.astype(BF16)

        def values(j, p_ref, a_ref):
            vtb = vt_ref[j]
            for half in range(2):
                acc_ref[half] = a_ref[half] * acc_ref[half] + _dot(vtb, p_ref[half])

        logits(0, s0_ref, x0_ref)

        def pair(jj, carry):
            j = 2 * jj
            logits(j + 1, s1_ref, x1_ref)
            softmax(j, s0_ref, x0_ref, p0_ref, a0_ref)
            values(jnp.maximum(j - 1, 0), p1_ref, a1_ref)
            logits(jnp.minimum(j + 2, n_kv - 1), s0_ref, x0_ref)
            softmax(j + 1, s1_ref, x1_ref, p1_ref, a1_ref)
            values(j, p0_ref, a0_ref)
            return carry

        lax.fori_loop(0, n_kv // 2, pair, 0)
        values(n_kv - 1, p1_ref, a1_ref)

    ot = acc_ref[0] * (1.0 / l_ref[0]) - scal_ref[0] * (acc_ref[1] * (1.0 / l_ref[1]))
    inv_rms = lax.rsqrt(jnp.mean(ot * ot, axis=0, keepdims=True) + RMS_EPS)
    ot = ot * inv_rms * (sg_ref[...] * out_scale)
    o_ref[...] = _dot_nt(eye_ref[...], ot.astype(BF16)).astype(BF16)


def _logit_bound(q_gain, k_gain):
    q_scale = HEAD_DIM ** -0.5 * LOG2E
    return (HEAD_DIM * q_scale * 1.02) * jnp.max(jnp.abs(q_gain.astype(F32))) * jnp.max(jnp.abs(k_gain.astype(F32)))


def _diff_attention(qk, vt, lam, bound, subln_g, eye, batch, seq, lambda_init):
    assert A_TQ % A_TK == 0 and seq % A_TQ == 0 and eye.shape == (A_TQ, A_TQ)
    assert (seq // A_TK) % 2 == 0
    m = qk.shape[0]
    n_q = seq // A_TQ
    n_kv = seq // A_TK
    slopes = jnp.asarray(2.0 ** (-8.0 * np.arange(1, A_HEADS + 1) / A_HEADS), F32)
    kern = functools.partial(_diff_attn_kernel, out_scale=1.0 - lambda_init)
    smem = pl.BlockSpec(memory_space=pltpu.SMEM)
    return pl.pallas_call(
        kern,
        grid=(batch, A_HEADS, n_q),
        in_specs=[
            smem,
            smem,
            pl.BlockSpec((A_TQ, A_VDIM), lambda b, h, i: (b * n_q + i, h)),
            pl.BlockSpec((seq, A_VDIM), lambda b, h, i: (b, A_HEADS + h)),
            pl.BlockSpec((n_kv, A_VDIM, A_TK), lambda b, h, i: (b, h, 0)),
            pl.BlockSpec((A_VDIM, 1), lambda b, h, i: (0, 0)),
            pl.BlockSpec((A_TQ, A_TQ), lambda b, h, i: (0, 0)),
        ],
        out_specs=pl.BlockSpec((A_TQ, A_VDIM), lambda b, h, i: (b * n_q + i, h)),
        out_shape=jax.ShapeDtypeStruct((m, A_HEADS * A_VDIM), BF16),
        scratch_shapes=[
            pltpu.VMEM((A_VDIM, A_TQ), BF16),
            pltpu.VMEM((2 + A_TQ // A_TK, A_TK, A_TQ), F32),
            pltpu.VMEM((2, A_TK, A_TQ), F32),
            pltpu.VMEM((2, A_TK, A_TQ), F32),
            pltpu.VMEM((2, 1, A_TQ), F32),
            pltpu.VMEM((2, 1, A_TQ), F32),
            pltpu.VMEM((2, A_TK, A_TQ), BF16),
            pltpu.VMEM((2, A_TK, A_TQ), BF16),
            pltpu.VMEM((2, 1, A_TQ), F32),
            pltpu.VMEM((2, 1, A_TQ), F32),
            pltpu.VMEM((2, 1, A_TQ), F32),
            pltpu.VMEM((2, 1, A_TQ), F32),
            pltpu.VMEM((2, A_VDIM, A_TQ), F32),
        ],
        compiler_params=_params(("parallel", "parallel", "arbitrary")),
        name="diff_attention",
    )(jnp.stack([lam, bound]).astype(F32), slopes, qk, qk, vt, subln_g.reshape(A_VDIM, 1), eye)


B_QROWS = 8
B_KROWS = 16
B_TQ = B_QROWS * GRID_W
B_TK = B_KROWS * GRID_W


def _na_bias_table(rel_bias, rows):
    kr = min(NA_ROWS, rows)
    n_groups = rows // B_QROWS
    n_heads, n_drow, n_dcol = rel_bias.shape
    c = np.arange(GRID_W)
    c_start = np.clip(c - NA_COLS // 2, 0, GRID_W - NA_COLS)
    kc = np.arange(GRID_W)
    col_ok = (kc[None, :] >= c_start[:, None]) & (kc[None, :] < c_start[:, None] + NA_COLS)
    dcol = kc[None, :] - c[:, None] + (NA_COLS - 1)
    select = (np.arange(n_dcol)[:, None, None] == dcol[None]) & col_ok[None]
    cols = jnp.einsum("hrd,dck->hcrk", rel_bias * LOG2E, jnp.asarray(select, F32),
                      precision=lax.Precision.HIGHEST)
    strip = jnp.where(col_ok[None, :, None, :], cols, NEG_BIG).reshape(n_heads, GRID_W, n_drow * GRID_W)
    row_blocks = []
    for g in (0, 1, n_groups - 1):
        k_row0 = int(np.clip(g * B_QROWS - kr // 2, 0, rows - B_KROWS))
        for r in range(g * B_QROWS, (g + 1) * B_QROWS):
            r_start = int(np.clip(r - kr // 2, 0, rows - kr))
            lead = r_start - k_row0
            d0 = r_start - r + (NA_ROWS - 1)
            parts = [jnp.full((n_heads, GRID_W, lead * GRID_W), NEG_BIG, F32),
                     strip[:, :, d0 * GRID_W:(d0 + kr) * GRID_W],
                     jnp.full((n_heads, GRID_W, (B_KROWS - kr - lead) * GRID_W), NEG_BIG, F32)]
            row_blocks.append(jnp.concatenate(parts, axis=-1))
    return jnp.stack(row_blocks, axis=1).reshape(n_heads, 3, B_TQ, B_TK)


def _na_kernel(q_ref, k_ref, v_ref, bias_ref, o_ref, *, rows):
    n_groups = rows // B_QROWS

    def group(g, carry):
        q0 = pl.multiple_of(g * B_TQ, B_TQ)
        k_row0 = jnp.clip(g * B_QROWS - NA_ROWS // 2, 0, rows - B_KROWS)
        k0 = pl.multiple_of(k_row0 * GRID_W, (NA_ROWS // 2) * GRID_W)
        variant = jnp.where(g == 0, 0, jnp.where(g == n_groups - 1, 2, 1))
        s = _dot_nt(q_ref[pl.ds(q0, B_TQ), :], k_ref[pl.ds(k0, B_TK), :]) + bias_ref[0, variant]
        p = jnp.exp2(s - jnp.max(s, axis=-1, keepdims=True))
        l = jnp.sum(p, axis=-1, keepdims=True)
        o = _dot(p.astype(BF16), v_ref[pl.ds(k0, B_TK), :])
        o_ref[pl.ds(q0, B_TQ), :] = (o / l).astype(BF16)
        return carry

    lax.fori_loop(0, n_groups, group, 0)


def _neighbourhood_attention(qkv, bias, batch, seq):
    m = qkv.shape[0]
    rows = seq // GRID_W
    assert rows % B_QROWS == 0 and rows >= B_KROWS and NA_ROWS <= rows
    kern = functools.partial(_na_kernel, rows=rows)
    return pl.pallas_call(
        kern,
        grid=(batch, B_HEADS),
        in_specs=[
            pl.BlockSpec((seq, HEAD_DIM), lambda b, h: (b, h)),
            pl.BlockSpec((seq, HEAD_DIM), lambda b, h: (b, B_HEADS + h)),
            pl.BlockSpec((seq, HEAD_DIM), lambda b, h: (b, 2 * B_HEADS + h)),
            pl.BlockSpec((1, 3, B_TQ, B_TK), lambda b, h: (h, 0, 0, 0)),
        ],
        out_specs=pl.BlockSpec((seq, HEAD_DIM), lambda b, h: (b, h)),
        out_shape=jax.ShapeDtypeStruct((m, B_HEADS * HEAD_DIM), BF16),
        compiler_params=_params(("parallel", "arbitrary")),
        name="neighbourhood_attention",
    )(qkv, qkv, qkv, bias)


C_TQ = 256
C_TK = C_TQ + 2 * C_WINDOW


def _swa_kernel(slope_ref, sink_ref, q_ref, k_ref, v_ref, o_ref):
    kv = pl.program_id(1)
    seq = k_ref.shape[0]
    rows = lax.broadcasted_iota(jnp.int32, (C_TQ, C_TK), 0)
    cols = lax.broadcasted_iota(jnp.int32, (C_TQ, C_TK), 1)
    delta = rows - cols

    def tile(i, carry):
        t0 = pl.multiple_of(i * C_TQ, C_TQ)
        k0 = pl.multiple_of(jnp.clip(t0 - C_WINDOW, 0, seq - C_TK), C_WINDOW)
        dist = jnp.abs(delta + (t0 - k0))
        valid = dist <= C_WINDOW
        dist_f = dist.astype(F32)
        kb = k_ref[pl.ds(k0, C_TK), :]
        vb = v_ref[pl.ds(k0, C_TK), :]
        for g in range(C_GROUP):
            head = kv * C_GROUP + g
            cols_g = slice(g * HEAD_DIM, (g + 1) * HEAD_DIM)
            s = _dot_nt(q_ref[pl.ds(t0, C_TQ), cols_g], kb)
            s = jnp.where(valid, s - (slope_ref[head] * LOG2E) * dist_f, NEG_BIG)
            sink = sink_ref[head] * LOG2E
            m = jnp.maximum(jnp.max(s, axis=-1, keepdims=True), sink)
            p = jnp.exp2(s - m)
            l = jnp.sum(p, axis=-1, keepdims=True) + jnp.exp2(sink - m)
            o = _dot(p.astype(BF16), vb)
            o_ref[pl.ds(t0, C_TQ), cols_g] = (o / l).astype(BF16)
        return carry

    lax.fori_loop(0, seq // C_TQ, tile, 0)


def _window_attention(qkv, sink, batch, seq):
    m = qkv.shape[0]
    assert seq % C_TQ == 0 and seq >= C_TK
    slopes = jnp.asarray(2.0 ** (-8.0 * np.arange(1, C_HEADS + 1) / C_HEADS), F32)
    gw = C_GROUP * HEAD_DIM
    smem = pl.BlockSpec(memory_space=pltpu.SMEM)
    return pl.pallas_call(
        _swa_kernel,
        grid=(batch, C_KV_HEADS),
        in_specs=[
            smem,
            smem,
            pl.BlockSpec((seq, gw), lambda b, kv: (b, kv)),
            pl.BlockSpec((seq, HEAD_DIM), lambda b, kv: (b, C_HEADS + kv)),
            pl.BlockSpec((seq, HEAD_DIM), lambda b, kv: (b, C_HEADS + C_KV_HEADS + kv)),
        ],
        out_specs=pl.BlockSpec((seq, gw), lambda b, kv: (b, kv)),
        out_shape=jax.ShapeDtypeStruct((m, C_HEADS * HEAD_DIM), BF16),
        compiler_params=_params(("parallel", "arbitrary")),
        name="window_attention",
    )(slopes, sink.astype(F32), qkv, qkv, qkv)


def _col_gain(q_gain, k_gain, n_q_heads, n_k_heads, n_cols):
    q_scale = HEAD_DIM ** -0.5 * LOG2E
    parts = [jnp.tile(q_gain.astype(F32) * q_scale, n_q_heads), jnp.tile(k_gain.astype(F32), n_k_heads)]
    gain = jnp.concatenate(parts)
    return jnp.pad(gain, (0, n_cols - gain.shape[0]), constant_values=1.0).reshape(1, n_cols)


def kernel(x, norm_ffn1, ffn1_w_gate, ffn1_w_up, ffn1_w_down, norm_mix, norm_ffn2, ffn2_w_gate, ffn2_w_up, ffn2_w_down, a_w_qkv, a_q_norm, a_k_norm, a_lambda_q1, a_lambda_k1, a_lambda_q2, a_lambda_k2, a_subln, a_w_o, b_w_qkv, b_q_norm, b_k_norm, b_rel_bias, b_w_o, c_w_qkv, c_q_norm, c_k_norm, c_sink, c_w_o):
    batch, seq, d = x.shape
    x = x.reshape(batch * seq, d)
    to_bf16 = lambda w: w.astype(BF16)
    ffn1 = tuple(map(to_bf16, (ffn1_w_gate, ffn1_w_up, ffn1_w_down)))
    ffn2 = tuple(map(to_bf16, (ffn2_w_gate, ffn2_w_up, ffn2_w_down)))
    a_w_qkv, a_w_o, b_w_qkv, b_w_o, c_w_qkv, c_w_o = map(
        to_bf16, (a_w_qkv, a_w_o, b_w_qkv, b_w_o, c_w_qkv, c_w_o))
    eye = jnp.eye(A_TQ, dtype=BF16)

    for i in range(DEPTH):
        x = _ffn(x, norm_ffn1[i], *ffn1, i)
        kind, j = i % N_MIXERS, i // N_MIXERS
        if kind == 0:
            lambda_init = 0.8 - 0.6 * math.exp(-0.3 * i)
            lam = (jnp.exp(jnp.sum(a_lambda_q1[j].astype(F32) * a_lambda_k1[j].astype(F32)))
                   - jnp.exp(jnp.sum(a_lambda_q2[j].astype(F32) * a_lambda_k2[j].astype(F32))) + lambda_init)
            n_qk = 2 * A_HEADS * 2 * HEAD_DIM
            gain = _col_gain(a_q_norm[j], a_k_norm[j], 2 * A_HEADS, 2 * A_HEADS, n_qk)
            qk = _qkv(x, norm_mix[i], a_w_qkv, j, gain, n_qk, n=n_qk)
            vt = _proj_t(x, norm_mix[i], a_w_qkv, j, n_qk, eye)
            bound = _logit_bound(a_q_norm[j], a_k_norm[j])
            o = _diff_attention(qk, vt, lam, bound, a_subln[j], eye, batch, seq, lambda_init)
            x = _out_proj(o, a_w_o, j, x)
        elif kind == 1:
            n_qk = 2 * B_HEADS * HEAD_DIM
            gain = _col_gain(b_q_norm[j], b_k_norm[j], B_HEADS, B_HEADS, b_w_qkv.shape[2])
            qkv = _qkv(x, norm_mix[i], b_w_qkv, j, gain, n_qk)
            bias = _na_bias_table(b_rel_bias[j].astype(F32), seq // GRID_W)
            o = _neighbourhood_attention(qkv, bias, batch, seq)
            x = _out_proj(o, b_w_o, j, x)
        else:
            n_qk = (C_HEADS + C_KV_HEADS) * HEAD_DIM
            gain = _col_gain(c_q_norm[j], c_k_norm[j], C_HEADS, C_KV_HEADS, c_w_qkv.shape[2])
            qkv = _qkv(x, norm_mix[i], c_w_qkv, j, gain, n_qk)
            o = _window_attention(qkv, c_sink[j], batch, seq)
            x = _out_proj(o, c_w_o, j, x)
        x = _ffn(x, norm_ffn2[i], *ffn2, i)
    return x.reshape(batch, seq, d)
```

```python
import functools
import math

import jax
import jax.numpy as jnp
import numpy as np
from jax import lax
from jax.experimental import pallas as pl
from jax.experimental.pallas import tpu as pltpu

D_MODEL = 2048
DEPTH = 4
N_MIXERS = 3
HEAD_DIM = 128
D_FF = 5632
RMS_EPS = 1e-6
A_HEADS = D_MODEL // (2 * HEAD_DIM)
A_VDIM = 2 * HEAD_DIM
B_HEADS = D_MODEL // HEAD_DIM
GRID_W = 64
NA_ROWS = 8
NA_COLS = 16
C_HEADS = D_MODEL // HEAD_DIM
C_KV_HEADS = 4
C_GROUP = C_HEADS // C_KV_HEADS
C_WINDOW = 128

LOG2E = math.log2(math.e)
NEG_BIG = -1e30
BF16 = jnp.bfloat16
F32 = jnp.float32

V7X_VMEM_BYTES = 64 * 1024 * 1024
VMEM_LIMIT = 56 * 1024 * 1024
LANE = 128


def _params(semantics):
    return pltpu.CompilerParams(dimension_semantics=semantics, vmem_limit_bytes=VMEM_LIMIT)


def _dot(a, b):
    return jnp.dot(a, b, preferred_element_type=F32)


def _dot_nt(a, b):
    return lax.dot_general(a, b, (((1,), (1,)), ((), ())), preferred_element_type=F32)


def _rms_normalise(x):
    return x * lax.rsqrt(jnp.mean(x * x, axis=-1, keepdims=True) + RMS_EPS)


FFN_TM = 1024
FFN_TF = 512


def _ffn_kernel(x_ref, g_ref, wg_ref, wu_ref, wd_ref, o_ref, h_ref):
    j = pl.program_id(1)

    @pl.when(j == 0)
    def _():
        x = x_ref[...]
        h_ref[...] = (_rms_normalise(x) * g_ref[...]).astype(BF16)
        o_ref[...] = x

    h = h_ref[...]
    gate = _dot(h, wg_ref[...])
    up = _dot(h, wu_ref[...])
    act = gate * (0.5 / (1.0 + jnp.exp(-gate))) * up
    o_ref[...] += _dot(act.astype(BF16), wd_ref[...])


def _ffn(x, g, wg, wu, wd, layer):
    m, d = x.shape
    f = wg.shape[2]
    return pl.pallas_call(
        _ffn_kernel,
        grid=(m // FFN_TM, f // FFN_TF),
        in_specs=[
            pl.BlockSpec((FFN_TM, d), lambda i, j: (i, 0)),
            pl.BlockSpec((1, d), lambda i, j: (0, 0)),
            pl.BlockSpec((None, d, FFN_TF), lambda i, j: (layer, 0, j)),
            pl.BlockSpec((None, d, FFN_TF), lambda i, j: (layer, 0, j)),
            pl.BlockSpec((None, FFN_TF, d), lambda i, j: (layer, j, 0)),
        ],
        out_specs=pl.BlockSpec((FFN_TM, d), lambda i, j: (i, 0)),
        out_shape=jax.ShapeDtypeStruct((m, d), F32),
        scratch_shapes=[pltpu.VMEM((FFN_TM, d), BF16)],
        compiler_params=_params(("parallel", "arbitrary")),
        name="macaron_ffn",
    )(x, g.reshape(1, d), wg, wu, wd)


PROJ_TM = 1024
PROJ_TN = 512


def _qkv_kernel(x_ref, g_ref, w_ref, cg_ref, o_ref, h_ref, *, n_norm_blocks):
    j = pl.program_id(1)

    @pl.when(j == 0)
    def _():
        h_ref[...] = (_rms_normalise(x_ref[...]) * g_ref[...]).astype(BF16)

    y = _dot(h_ref[...], w_ref[...])

    @pl.when(j < n_norm_blocks)
    def _():
        for c in range(PROJ_TN // HEAD_DIM):
            cols = slice(c * HEAD_DIM, (c + 1) * HEAD_DIM)
            o_ref[:, cols] = (_rms_normalise(y[:, cols]) * cg_ref[:, cols]).astype(BF16)

    @pl.when(j >= n_norm_blocks)
    def _():
        o_ref[...] = y.astype(BF16)


def _qkv(x, g, w, layer, col_gain, n_norm_cols, n=None):
    m, d = x.shape
    n = w.shape[2] if n is None else n
    kern = functools.partial(_qkv_kernel, n_norm_blocks=n_norm_cols // PROJ_TN)
    return pl.pallas_call(
        kern,
        grid=(m // PROJ_TM, n // PROJ_TN),
        in_specs=[
            pl.BlockSpec((PROJ_TM, d), lambda i, j: (i, 0)),
            pl.BlockSpec((1, d), lambda i, j: (0, 0)),
            pl.BlockSpec((None, d, PROJ_TN), lambda i, j: (layer, 0, j)),
            pl.BlockSpec((1, PROJ_TN), lambda i, j: (0, j)),
        ],
        out_specs=pl.BlockSpec((PROJ_TM, PROJ_TN), lambda i, j: (i, j)),
        out_shape=jax.ShapeDtypeStruct((m, n), BF16),
        scratch_shapes=[pltpu.VMEM((PROJ_TM, d), BF16)],
        compiler_params=_params(("parallel", "arbitrary")),
        name="mixer_qkv",
    )(x, g.reshape(1, d), w, col_gain)


A_TK = 512


def _proj_t_kernel(x_ref, g_ref, w_ref, eye_ref, o_ref, h_ref):
    @pl.when(pl.program_id(1) == 0)
    def _():
        h_ref[...] = (_rms_normalise(x_ref[...]) * g_ref[...]).astype(BF16)

    y = _dot(h_ref[...], w_ref[...]).astype(BF16)
    yt = _dot_nt(eye_ref[...], y).astype(BF16)
    for r in range(PROJ_TM // A_TK):
        o_ref[r] = yt[:, r * A_TK:(r + 1) * A_TK]


def _proj_t(x, g, w, layer, col0, eye):
    m, d = x.shape
    n = w.shape[2] - col0
    j0 = col0 // PROJ_TN
    return pl.pallas_call(
        _proj_t_kernel,
        grid=(m // PROJ_TM, n // PROJ_TN),
        in_specs=[
            pl.BlockSpec((PROJ_TM, d), lambda i, j: (i, 0)),
            pl.BlockSpec((1, d), lambda i, j: (0, 0)),
            pl.BlockSpec((None, d, PROJ_TN), lambda i, j: (layer, 0, j0 + j)),
            pl.BlockSpec((PROJ_TN, PROJ_TN), lambda i, j: (0, 0)),
        ],
        out_specs=pl.BlockSpec((PROJ_TM // A_TK, PROJ_TN, A_TK), lambda i, j: (i, j, 0)),
        out_shape=jax.ShapeDtypeStruct((m // A_TK, n, A_TK), BF16),
        scratch_shapes=[pltpu.VMEM((PROJ_TM, d), BF16)],
        compiler_params=_params(("parallel", "arbitrary")),
        name="mixer_v_transposed",
    )(x, g.reshape(1, d), w, eye)


def _out_proj_kernel(a_ref, w_ref, x_ref, o_ref):
    o_ref[...] = x_ref[...] + _dot(a_ref[...], w_ref[...])


OUT_TM = 512


def _out_proj(a, w, layer, x):
    m, k = a.shape
    n = w.shape[2]
    return pl.pallas_call(
        _out_proj_kernel,
        grid=(m // OUT_TM,),
        in_specs=[
            pl.BlockSpec((OUT_TM, k), lambda i: (i, 0)),
            pl.BlockSpec((None, k, n), lambda i: (layer, 0, 0)),
            pl.BlockSpec((OUT_TM, n), lambda i: (i, 0)),
        ],
        out_specs=pl.BlockSpec((OUT_TM, n), lambda i: (i, 0)),
        out_shape=jax.ShapeDtypeStruct((m, n), F32),
        compiler_params=_params(("parallel",)),
        name="mixer_out_proj",
    )(a, w, x)


A_TQ = 512
A_MAX_FIXED_SHIFT = 40.0


def _diff_attn_kernel(scal_ref, slope_ref, q_ref, k_ref, vt_ref, sg_ref, eye_ref, o_ref,
                      qt_ref, tbl_ref, s0_ref, s1_ref, x0_ref, x1_ref, p0_ref, p1_ref, a0_ref, a1_ref,
                      m_ref, l_ref, acc_ref, *, out_scale):
    h = pl.program_id(1)
    qi = pl.program_id(2)
    n_kv = vt_ref.shape[0]
    kv_per_q = A_TQ // A_TK
    c = slope_ref[h] * LOG2E

    @pl.when(qi == 0)
    def _():
        rows = lax.broadcasted_iota(jnp.int32, (A_TK, A_TQ), 0)
        cols = lax.broadcasted_iota(jnp.int32, (A_TK, A_TQ), 1)
        cd = c * (rows - cols).astype(F32)
        tbl_ref[0] = cd
        tbl_ref[1] = -cd
        for d in range(kv_per_q):
            tbl_ref[2 + d] = -jnp.abs(cd + c * float(d * A_TK))

    qt_ref[...] = _dot_nt(eye_ref[:A_VDIM, :A_VDIM], q_ref[...]).astype(BF16)
    m_ref[...] = jnp.full(m_ref.shape, NEG_BIG, F32)
    l_ref[...] = jnp.zeros(l_ref.shape, F32)
    acc_ref[...] = jnp.zeros(acc_ref.shape, F32)
    p1_ref[...] = jnp.zeros(p1_ref.shape, BF16)
    a1_ref[...] = jnp.ones(a1_ref.shape, F32)

    def bias_of(j):
        d = j - qi * kv_per_q
        off = c * (d * A_TK).astype(F32)
        idx = jnp.where(d < 0, 0, jnp.where(d >= kv_per_q, 1, 2 + d))
        kappa = jnp.where(d < 0, off, jnp.where(d >= kv_per_q, -off, 0.0))
        return idx, kappa

    def biased_logits(j, half, idx):
        k0 = pl.multiple_of(j * A_TK, A_TK)
        cols_h = slice(half * HEAD_DIM, (half + 1) * HEAD_DIM)
        return _dot(k_ref[pl.ds(k0, A_TK), cols_h], qt_ref[cols_h, :]) + tbl_ref[idx]

    bound = scal_ref[1]
    fixed_shift_ok = bound <= A_MAX_FIXED_SHIFT

    @pl.when(fixed_shift_ok)
    def _():
        def probabilities(j, p_ref, live):
            idx, kappa = bias_of(j)
            shift = kappa - bound
            for half in range(2):
                p = jnp.exp2(biased_logits(j, half, idx) + shift)
                l_ref[half] += live * jnp.sum(p, axis=0, keepdims=True)
                p_ref[half] = p.astype(BF16)

        def values(j, p_ref):
            vtb = vt_ref[j]
            for half in range(2):
                acc_ref[half] += _dot(vtb, p_ref[half])

        probabilities(0, p0_ref, 1.0)

        def pair(jj, carry):
            j = 2 * jj
            probabilities(j + 1, p1_ref, 1.0)
            values(j, p0_ref)
            nxt = j + 2
            probabilities(jnp.minimum(nxt, n_kv - 1), p0_ref, (nxt < n_kv).astype(F32))
            values(j + 1, p1_ref)
            return carry

        lax.fori_loop(0, n_kv // 2, pair, 0)

    @pl.when(jnp.logical_not(fixed_shift_ok))
    def _():
        def logits(j, z_ref, zmax_ref):
            idx, _ = bias_of(j)
            for half in range(2):
                z = biased_logits(j, half, idx)
                z_ref[half] = z
                zmax_ref[half] = jnp.max(z, axis=0, keepdims=True)

        def softmax(j, z_ref, zmax_ref, p_ref, a_ref):
            _, kappa = bias_of(j)
            for half in range(2):
                m_old = m_ref[half]
                m_new = jnp.maximum(m_old, zmax_ref[half] + kappa)
                p = jnp.exp2(z_ref[half] - (m_new - kappa))
                alpha = jnp.exp2(m_old - m_new)
                l_ref[half] = alpha * l_ref[half] + jnp.sum(p, axis=0, keepdims=True)
                m_ref[half] = m_new
                a_ref[half] = alpha
                p_ref[half] = p.astype(BF16)

        def values(j, p_ref, a_ref):
            vtb = vt_ref[j]
            for half in range(2):
                acc_ref[half] = a_ref[half] * acc_ref[half] + _dot(vtb, p_ref[half])

        logits(0, s0_ref, x0_ref)

        def pair(jj, carry):
            j = 2 * jj
            logits(j + 1, s1_ref, x1_ref)
            softmax(j, s0_ref, x0_ref, p0_ref, a0_ref)
            values(jnp.maximum(j - 1, 0), p1_ref, a1_ref)
            logits(jnp.minimum(j + 2, n_kv - 1), s0_ref, x0_ref)
            softmax(j + 1, s1_ref, x1_ref, p1_ref, a1_ref)
            values(j, p0_ref, a0_ref)
            return carry

        lax.fori_loop(0, n_kv // 2, pair, 0)
        values(n_kv - 1, p1_ref, a1_ref)

    ot = acc_ref[0] * (1.0 / l_ref[0]) - scal_ref[0] * (acc_ref[1] * (1.0 / l_ref[1]))
    inv_rms = lax.rsqrt(jnp.mean(ot * ot, axis=0, keepdims=True) + RMS_EPS)
    ot = ot * inv_rms * (sg_ref[...] * out_scale)
    o_ref[...] = _dot_nt(eye_ref[...], ot.astype(BF16)).astype(BF16)


def _logit_bound(q_gain, k_gain):
    q_scale = HEAD_DIM ** -0.5 * LOG2E
    return (HEAD_DIM * q_scale * 1.02) * jnp.max(jnp.abs(q_gain.astype(F32))) * jnp.max(jnp.abs(k_gain.astype(F32)))


def _diff_attention(qk, vt, lam, bound, subln_g, eye, batch, seq, lambda_init):
    assert A_TQ % A_TK == 0 and seq % A_TQ == 0 and eye.shape == (A_TQ, A_TQ)
    assert (seq // A_TK) % 2 == 0
    m = qk.shape[0]
    n_q = seq // A_TQ
    n_kv = seq // A_TK
    slopes = jnp.asarray(2.0 ** (-8.0 * np.arange(1, A_HEADS + 1) / A_HEADS), F32)
    kern = functools.partial(_diff_attn_kernel, out_scale=1.0 - lambda_init)
    smem = pl.BlockSpec(memory_space=pltpu.SMEM)
    return pl.pallas_call(
        kern,
        grid=(batch, A_HEADS, n_q),
        in_specs=[
            smem,
            smem,
            pl.BlockSpec((A_TQ, A_VDIM), lambda b, h, i: (b * n_q + i, h)),
            pl.BlockSpec((seq, A_VDIM), lambda b, h, i: (b, A_HEADS + h)),
            pl.BlockSpec((n_kv, A_VDIM, A_TK), lambda b, h, i: (b, h, 0)),
            pl.BlockSpec((A_VDIM, 1), lambda b, h, i: (0, 0)),
            pl.BlockSpec((A_TQ, A_TQ), lambda b, h, i: (0, 0)),
        ],
        out_specs=pl.BlockSpec((A_TQ, A_VDIM), lambda b, h, i: (b * n_q + i, h)),
        out_shape=jax.ShapeDtypeStruct((m, A_HEADS * A_VDIM), BF16),
        scratch_shapes=[
            pltpu.VMEM((A_VDIM, A_TQ), BF16),
            pltpu.VMEM((2 + A_TQ // A_TK, A_TK, A_TQ), F32),
            pltpu.VMEM((2, A_TK, A_TQ), F32),
            pltpu.VMEM((2, A_TK, A_TQ), F32),
            pltpu.VMEM((2, 1, A_TQ), F32),
            pltpu.VMEM((2, 1, A_TQ), F32),
            pltpu.VMEM((2, A_TK, A_TQ), BF16),
            pltpu.VMEM((2, A_TK, A_TQ), BF16),
            pltpu.VMEM((2, 1, A_TQ), F32),
            pltpu.VMEM((2, 1, A_TQ), F32),
            pltpu.VMEM((2, 1, A_TQ), F32),
            pltpu.VMEM((2, 1, A_TQ), F32),
            pltpu.VMEM((2, A_VDIM, A_TQ), F32),
        ],
        compiler_params=_params(("parallel", "parallel", "arbitrary")),
        name="diff_attention",
    )(jnp.stack([lam, bound]).astype(F32), slopes, qk, qk, vt, subln_g.reshape(A_VDIM, 1), eye)


B_QROWS = 8
B_KROWS = 16
B_TQ = B_QROWS * GRID_W
B_TK = B_KROWS * GRID_W


def _na_bias_table(rel_bias, rows):
    kr = min(NA_ROWS, rows)
    n_groups = rows // B_QROWS
    n_heads, n_drow, n_dcol = rel_bias.shape
    c = np.arange(GRID_W)
    c_start = np.clip(c - NA_COLS // 2, 0, GRID_W - NA_COLS)
    kc = np.arange(GRID_W)
    col_ok = (kc[None, :] >= c_start[:, None]) & (kc[None, :] < c_start[:, None] + NA_COLS)
    dcol = kc[None, :] - c[:, None] + (NA_COLS - 1)
    select = (np.arange(n_dcol)[:, None, None] == dcol[None]) & col_ok[None]
    cols = jnp.einsum("hrd,dck->hcrk", rel_bias * LOG2E, jnp.asarray(select, F32),
                      precision=lax.Precision.HIGHEST)
    strip = jnp.where(col_ok[None, :, None, :], cols, NEG_BIG).reshape(n_heads, GRID_W, n_drow * GRID_W)
    row_blocks = []
    for g in (0, 1, n_groups - 1):
        k_row0 = int(np.clip(g * B_QROWS - kr // 2, 0, rows - B_KROWS))
        for r in range(g * B_QROWS, (g + 1) * B_QROWS):
            r_start = int(np.clip(r - kr // 2, 0, rows - kr))
            lead = r_start - k_row0
            d0 = r_start - r + (NA_ROWS - 1)
            parts = [jnp.full((n_heads, GRID_W, lead * GRID_W), NEG_BIG, F32),
                     strip[:, :, d0 * GRID_W:(d0 + kr) * GRID_W],
                     jnp.full((n_heads, GRID_W, (B_KROWS - kr - lead) * GRID_W), NEG_BIG, F32)]
            row_blocks.append(jnp.concatenate(parts, axis=-1))
    return jnp.stack(row_blocks, axis=1).reshape(n_heads, 3, B_TQ, B_TK)


def _na_kernel(q_ref, k_ref, v_ref, bias_ref, o_ref, p0_ref, p1_ref, l0_ref, l1_ref, *, rows):
    n_groups = rows // B_QROWS

    def key_start(g):
        k_row0 = jnp.clip(g * B_QROWS - NA_ROWS // 2, 0, rows - B_KROWS)
        return pl.multiple_of(k_row0 * GRID_W, (NA_ROWS // 2) * GRID_W)

    def probabilities(g, p_ref, l_ref):
        q0 = pl.multiple_of(g * B_TQ, B_TQ)
        variant = jnp.where(g == 0, 0, jnp.where(g == n_groups - 1, 2, 1))
        s = _dot_nt(q_ref[pl.ds(q0, B_TQ), :], k_ref[pl.ds(key_start(g), B_TK), :]) + bias_ref[0, variant]
        p = jnp.exp2(s - jnp.max(s, axis=-1, keepdims=True))
        l_ref[...] = jnp.sum(p, axis=-1, keepdims=True)
        p_ref[...] = p.astype(BF16)

    def values(g, p_ref, l_ref):
        q0 = pl.multiple_of(g * B_TQ, B_TQ)
        o = _dot(p_ref[...], v_ref[pl.ds(key_start(g), B_TK), :])
        o_ref[pl.ds(q0, B_TQ), :] = (o * (1.0 / l_ref[...])).astype(BF16)

    probabilities(0, p0_ref, l0_ref)

    def pair(gg, carry):
        g = 2 * gg
        probabilities(g + 1, p1_ref, l1_ref)
        values(g, p0_ref, l0_ref)
        probabilities(jnp.minimum(g + 2, n_groups - 1), p0_ref, l0_ref)
        values(g + 1, p1_ref, l1_ref)
        return carry

    lax.fori_loop(0, n_groups // 2, pair, 0)


def _neighbourhood_attention(qkv, bias, batch, seq):
    m = qkv.shape[0]
    rows = seq // GRID_W
    assert rows % (2 * B_QROWS) == 0 and rows >= B_KROWS and NA_ROWS <= rows
    kern = functools.partial(_na_kernel, rows=rows)
    return pl.pallas_call(
        kern,
        grid=(batch, B_HEADS),
        in_specs=[
            pl.BlockSpec((seq, HEAD_DIM), lambda b, h: (b, h)),
            pl.BlockSpec((seq, HEAD_DIM), lambda b, h: (b, B_HEADS + h)),
            pl.BlockSpec((seq, HEAD_DIM), lambda b, h: (b, 2 * B_HEADS + h)),
            pl.BlockSpec((1, 3, B_TQ, B_TK), lambda b, h: (h, 0, 0, 0)),
        ],
        out_specs=pl.BlockSpec((seq, HEAD_DIM), lambda b, h: (b, h)),
        out_shape=jax.ShapeDtypeStruct((m, B_HEADS * HEAD_DIM), BF16),
        scratch_shapes=[
            pltpu.VMEM((B_TQ, B_TK), BF16),
            pltpu.VMEM((B_TQ, B_TK), BF16),
            pltpu.VMEM((B_TQ, 1), F32),
            pltpu.VMEM((B_TQ, 1), F32),
        ],
        compiler_params=_params(("parallel", "arbitrary")),
        name="neighbourhood_attention",
    )(qkv, qkv, qkv, bias)


C_TQ = 256
C_TK = C_TQ + 2 * C_WINDOW


def _swa_kernel(slope_ref, sink_ref, q_ref, k_ref, v_ref, o_ref, p0_ref, p1_ref, l0_ref, l1_ref):
    kv = pl.program_id(1)
    seq = k_ref.shape[0]
    n_tiles = seq // C_TQ
    rows = lax.broadcasted_iota(jnp.int32, (C_TQ, C_TK), 0)
    cols = lax.broadcasted_iota(jnp.int32, (C_TQ, C_TK), 1)
    delta = rows - cols

    def key_start(t0):
        return pl.multiple_of(jnp.clip(t0 - C_WINDOW, 0, seq - C_TK), C_WINDOW)

    def probabilities(i, p_ref, l_ref):
        t0 = pl.multiple_of(i * C_TQ, C_TQ)
        k0 = key_start(t0)
        dist = jnp.abs(delta + (t0 - k0))
        valid = dist <= C_WINDOW
        dist_f = dist.astype(F32)
        kb = k_ref[pl.ds(k0, C_TK), :]
        for g in range(C_GROUP):
            head = kv * C_GROUP + g
            cols_g = slice(g * HEAD_DIM, (g + 1) * HEAD_DIM)
            s = _dot_nt(q_ref[pl.ds(t0, C_TQ), cols_g], kb)
            s = jnp.where(valid, s - (slope_ref[head] * LOG2E) * dist_f, NEG_BIG)
            sink = sink_ref[head] * LOG2E
            m = jnp.maximum(jnp.max(s, axis=-1, keepdims=True), sink)
            p = jnp.exp2(s - m)
            l_ref[g] = jnp.sum(p, axis=-1, keepdims=True) + jnp.exp2(sink - m)
            p_ref[g] = p.astype(BF16)

    def values(i, p_ref, l_ref):
        t0 = pl.multiple_of(i * C_TQ, C_TQ)
        vb = v_ref[pl.ds(key_start(t0), C_TK), :]
        for g in range(C_GROUP):
            cols_g = slice(g * HEAD_DIM, (g + 1) * HEAD_DIM)
            o_ref[pl.ds(t0, C_TQ), cols_g] = (_dot(p_ref[g], vb) * (1.0 / l_ref[g])).astype(BF16)

    probabilities(0, p0_ref, l0_ref)

    def pair(ii, carry):
        i = 2 * ii
        probabilities(i + 1, p1_ref, l1_ref)
        values(i, p0_ref, l0_ref)
        probabilities(jnp.minimum(i + 2, n_tiles - 1), p0_ref, l0_ref)
        values(i + 1, p1_ref, l1_ref)
        return carry

    lax.fori_loop(0, n_tiles // 2, pair, 0)


def _window_attention(qkv, sink, batch, seq):
    m = qkv.shape[0]
    assert seq % C_TQ == 0 and seq >= C_TK
    slopes = jnp.asarray(2.0 ** (-8.0 * np.arange(1, C_HEADS + 1) / C_HEADS), F32)
    gw = C_GROUP * HEAD_DIM
    smem = pl.BlockSpec(memory_space=pltpu.SMEM)
    return pl.pallas_call(
        _swa_kernel,
        grid=(batch, C_KV_HEADS),
        in_specs=[
            smem,
            smem,
            pl.BlockSpec((seq, gw), lambda b, kv: (b, kv)),
            pl.BlockSpec((seq, HEAD_DIM), lambda b, kv: (b, C_HEADS + kv)),
            pl.BlockSpec((seq, HEAD_DIM), lambda b, kv: (b, C_HEADS + C_KV_HEADS + kv)),
        ],
        out_specs=pl.BlockSpec((seq, gw), lambda b, kv: (b, kv)),
        out_shape=jax.ShapeDtypeStruct((m, C_HEADS * HEAD_DIM), BF16),
        scratch_shapes=[
            pltpu.VMEM((C_GROUP, C_TQ, C_TK), BF16),
            pltpu.VMEM((C_GROUP, C_TQ, C_TK), BF16),
            pltpu.VMEM((C_GROUP, C_TQ, 1), F32),
            pltpu.VMEM((C_GROUP, C_TQ, 1), F32),
        ],
        compiler_params=_params(("parallel", "arbitrary")),
        name="window_attention",
    )(slopes, sink.astype(F32), qkv, qkv, qkv)


def _col_gain(q_gain, k_gain, n_q_heads, n_k_heads, n_cols):
    q_scale = HEAD_DIM ** -0.5 * LOG2E
    parts = [jnp.tile(q_gain.astype(F32) * q_scale, n_q_heads), jnp.tile(k_gain.astype(F32), n_k_heads)]
    gain = jnp.concatenate(parts)
    return jnp.pad(gain, (0, n_cols - gain.shape[0]), constant_values=1.0).reshape(1, n_cols)


def kernel(x, norm_ffn1, ffn1_w_gate, ffn1_w_up, ffn1_w_down, norm_mix, norm_ffn2, ffn2_w_gate, ffn2_w_up, ffn2_w_down, a_w_qkv, a_q_norm, a_k_norm, a_lambda_q1, a_lambda_k1, a_lambda_q2, a_lambda_k2, a_subln, a_w_o, b_w_qkv, b_q_norm, b_k_norm, b_rel_bias, b_w_o, c_w_qkv, c_q_norm, c_k_norm, c_sink, c_w_o):
    batch, seq, d = x.shape
    x = x.reshape(batch * seq, d)
    to_bf16 = lambda w: w.astype(BF16)
    ffn1 = tuple(map(to_bf16, (ffn1_w_gate, ffn1_w_up, ffn1_w_down)))
    ffn2 = tuple(map(to_bf16, (ffn2_w_gate, ffn2_w_up, ffn2_w_down)))
    a_w_qkv, a_w_o, b_w_qkv, b_w_o, c_w_qkv, c_w_o = map(
        to_bf16, (a_w_qkv, a_w_o, b_w_qkv, b_w_o, c_w_qkv, c_w_o))
    eye = jnp.eye(A_TQ, dtype=BF16)

    for i in range(DEPTH):
        x = _ffn(x, norm_ffn1[i], *ffn1, i)
        kind, j = i % N_MIXERS, i // N_MIXERS
        if kind == 0:
            lambda_init = 0.8 - 0.6 * math.exp(-0.3 * i)
            lam = (jnp.exp(jnp.sum(a_lambda_q1[j].astype(F32) * a_lambda_k1[j].astype(F32)))
                   - jnp.exp(jnp.sum(a_lambda_q2[j].astype(F32) * a_lambda_k2[j].astype(F32))) + lambda_init)
            n_qk = 2 * A_HEADS * 2 * HEAD_DIM
            gain = _col_gain(a_q_norm[j], a_k_norm[j], 2 * A_HEADS, 2 * A_HEADS, n_qk)
            qk = _qkv(x, norm_mix[i], a_w_qkv, j, gain, n_qk, n=n_qk)
            vt = _proj_t(x, norm_mix[i], a_w_qkv, j, n_qk, eye)
            bound = _logit_bound(a_q_norm[j], a_k_norm[j])
            o = _diff_attention(qk, vt, lam, bound, a_subln[j], eye, batch, seq, lambda_init)
            x = _out_proj(o, a_w_o, j, x)
        elif kind == 1:
            n_qk = 2 * B_HEADS * HEAD_DIM
            gain = _col_gain(b_q_norm[j], b_k_norm[j], B_HEADS, B_HEADS, b_w_qkv.shape[2])
            qkv = _qkv(x, norm_mix[i], b_w_qkv, j, gain, n_qk)
            bias = _na_bias_table(b_rel_bias[j].astype(F32), seq // GRID_W)
            o = _neighbourhood_attention(qkv, bias, batch, seq)
            x = _out_proj(o, b_w_o, j, x)
        else:
            n_qk = (C_HEADS + C_KV_HEADS) * HEAD_DIM
            gain = _col_gain(c_q_norm[j], c_k_norm[j], C_HEADS, C_KV_HEADS, c_w_qkv.shape[2])
            qkv = _qkv(x, norm_mix[i], c_w_qkv, j, gain, n_qk)
            o = _window_attention(qkv, c_sink[j], batch, seq)
            x = _out_proj(o, c_w_o, j, x)
        x = _ffn(x, norm_ffn2[i], *ffn2, i)
    return x.reshape(batch, seq, d)
```

```python
import functools
import math

import jax
import jax.numpy as jnp
import numpy as np
from jax import lax
from jax.experimental import pallas as pl
from jax.experimental.pallas import tpu as pltpu

D_MODEL = 2048
DEPTH = 4
N_MIXERS = 3
HEAD_DIM = 128
D_FF = 5632
RMS_EPS = 1e-6
A_HEADS = D_MODEL // (2 * HEAD_DIM)
A_VDIM = 2 * HEAD_DIM
B_HEADS = D_MODEL // HEAD_DIM
GRID_W = 64
NA_ROWS = 8
NA_COLS = 16
C_HEADS = D_MODEL // HEAD_DIM
C_KV_HEADS = 4
C_GROUP = C_HEADS // C_KV_HEADS
C_WINDOW = 128

LOG2E = math.log2(math.e)
NEG_BIG = -1e30
BF16 = jnp.bfloat16
F32 = jnp.float32

V7X_VMEM_BYTES = 64 * 1024 * 1024
VMEM_LIMIT = 56 * 1024 * 1024
LANE = 128


def _params(semantics):
    return pltpu.CompilerParams(dimension_semantics=semantics, vmem_limit_bytes=VMEM_LIMIT)


def _dot(a, b):
    return jnp.dot(a, b, preferred_element_type=F32)


def _dot_nt(a, b):
    return lax.dot_general(a, b, (((1,), (1,)), ((), ())), preferred_element_type=F32)


def _rms_normalise(x):
    return x * lax.rsqrt(jnp.mean(x * x, axis=-1, keepdims=True) + RMS_EPS)


FFN_TM = 1024
FFN_TF = 512


def _ffn_kernel(x_ref, g_ref, wg_ref, wu_ref, wd_ref, o_ref, h_ref):
    j = pl.program_id(1)

    @pl.when(j == 0)
    def _():
        x = x_ref[...]
        h_ref[...] = (_rms_normalise(x) * g_ref[...]).astype(BF16)
        o_ref[...] = x

    h = h_ref[...]
    gate = _dot(h, wg_ref[...])
    up = _dot(h, wu_ref[...])
    act = gate * (0.5 / (1.0 + jnp.exp(-gate))) * up
    o_ref[...] += _dot(act.astype(BF16), wd_ref[...])


def _ffn(x, g, wg, wu, wd, layer):
    m, d = x.shape
    f = wg.shape[2]
    return pl.pallas_call(
        _ffn_kernel,
        grid=(m // FFN_TM, f // FFN_TF),
        in_specs=[
            pl.BlockSpec((FFN_TM, d), lambda i, j: (i, 0)),
            pl.BlockSpec((1, d), lambda i, j: (0, 0)),
            pl.BlockSpec((None, d, FFN_TF), lambda i, j: (layer, 0, j)),
            pl.BlockSpec((None, d, FFN_TF), lambda i, j: (layer, 0, j)),
            pl.BlockSpec((None, FFN_TF, d), lambda i, j: (layer, j, 0)),
        ],
        out_specs=pl.BlockSpec((FFN_TM, d), lambda i, j: (i, 0)),
        out_shape=jax.ShapeDtypeStruct((m, d), F32),
        scratch_shapes=[pltpu.VMEM((FFN_TM, d), BF16)],
        compiler_params=_params(("parallel", "arbitrary")),
        name="macaron_ffn",
    )(x, g.reshape(1, d), wg, wu, wd)


PROJ_TM = 1024
PROJ_TN = 512
QKV_SUB_BLOCKS = 4


def _qkv_kernel(x_ref, g_ref, w_ref, cg_ref, o_ref, h_ref, *, n_norm_steps, sub_blocks):
    j = pl.program_id(1)

    @pl.when(j == 0)
    def _():
        h_ref[...] = (_rms_normalise(x_ref[...]) * g_ref[...]).astype(BF16)

    def project(blk):
        cols = slice(blk * PROJ_TN, (blk + 1) * PROJ_TN)
        return _dot(h_ref[...], w_ref[:, cols])

    @pl.when(j < n_norm_steps)
    def _():
        for blk in range(sub_blocks):
            y = project(blk)
            for c in range(PROJ_TN // HEAD_DIM):
                cols = slice(blk * PROJ_TN + c * HEAD_DIM, blk * PROJ_TN + (c + 1) * HEAD_DIM)
                head = slice(c * HEAD_DIM, (c + 1) * HEAD_DIM)
                o_ref[:, cols] = (_rms_normalise(y[:, head]) * cg_ref[:, cols]).astype(BF16)

    @pl.when(j >= n_norm_steps)
    def _():
        for blk in range(sub_blocks):
            o_ref[:, blk * PROJ_TN:(blk + 1) * PROJ_TN] = project(blk).astype(BF16)


def _qkv(x, g, w, layer, col_gain, n_norm_cols, n=None, sub_blocks=1):
    m, d = x.shape
    n = w.shape[2] if n is None else n
    tn = sub_blocks * PROJ_TN
    assert n % tn == 0 and n_norm_cols % tn == 0
    kern = functools.partial(_qkv_kernel, n_norm_steps=n_norm_cols // tn, sub_blocks=sub_blocks)
    return pl.pallas_call(
        kern,
        grid=(m // PROJ_TM, n // tn),
        in_specs=[
            pl.BlockSpec((PROJ_TM, d), lambda i, j: (i, 0)),
            pl.BlockSpec((1, d), lambda i, j: (0, 0)),
            pl.BlockSpec((None, d, tn), lambda i, j: (layer, 0, j)),
            pl.BlockSpec((1, tn), lambda i, j: (0, j)),
        ],
        out_specs=pl.BlockSpec((PROJ_TM, tn), lambda i, j: (i, j)),
        out_shape=jax.ShapeDtypeStruct((m, n), BF16),
        scratch_shapes=[pltpu.VMEM((PROJ_TM, d), BF16)],
        compiler_params=_params(("parallel", "arbitrary")),
        name="mixer_qkv",
    )(x, g.reshape(1, d), w, col_gain)


A_TK = 512


def _proj_t_kernel(x_ref, g_ref, w_ref, o_ref, h_ref):
    @pl.when(pl.program_id(1) == 0)
    def _():
        h_ref[...] = (_rms_normalise(x_ref[...]) * g_ref[...]).astype(BF16)

    yt = _dot(h_ref[...], w_ref[...]).T.astype(BF16)
    for r in range(PROJ_TM // A_TK):
        o_ref[r] = yt[:, r * A_TK:(r + 1) * A_TK]


def _proj_t(x, g, w, layer, col0):
    m, d = x.shape
    n = w.shape[2] - col0
    j0 = col0 // PROJ_TN
    return pl.pallas_call(
        _proj_t_kernel,
        grid=(m // PROJ_TM, n // PROJ_TN),
        in_specs=[
            pl.BlockSpec((PROJ_TM, d), lambda i, j: (i, 0)),
            pl.BlockSpec((1, d), lambda i, j: (0, 0)),
            pl.BlockSpec((None, d, PROJ_TN), lambda i, j: (layer, 0, j0 + j)),
        ],
        out_specs=pl.BlockSpec((PROJ_TM // A_TK, PROJ_TN, A_TK), lambda i, j: (i, j, 0)),
        out_shape=jax.ShapeDtypeStruct((m // A_TK, n, A_TK), BF16),
        scratch_shapes=[pltpu.VMEM((PROJ_TM, d), BF16)],
        compiler_params=_params(("parallel", "arbitrary")),
        name="mixer_v_transposed",
    )(x, g.reshape(1, d), w)


def _out_proj_kernel(a_ref, w_ref, x_ref, o_ref):
    o_ref[...] = x_ref[...] + _dot(a_ref[...], w_ref[...])


OUT_TM = 512


def _out_proj(a, w, layer, x):
    m, k = a.shape
    n = w.shape[2]
    return pl.pallas_call(
        _out_proj_kernel,
        grid=(m // OUT_TM,),
        in_specs=[
            pl.BlockSpec((OUT_TM, k), lambda i: (i, 0)),
            pl.BlockSpec((None, k, n), lambda i: (layer, 0, 0)),
            pl.BlockSpec((OUT_TM, n), lambda i: (i, 0)),
        ],
        out_specs=pl.BlockSpec((OUT_TM, n), lambda i: (i, 0)),
        out_shape=jax.ShapeDtypeStruct((m, n), F32),
        compiler_params=_params(("parallel",)),
        name="mixer_out_proj",
    )(a, w, x)


A_TQ = 512
A_MAX_FIXED_SHIFT = 40.0


def _diff_attn_kernel(scal_ref, slope_ref, q_ref, k_ref, vt_ref, sg_ref, o_ref,
                      qt_ref, tbl_ref, s0_ref, s1_ref, x0_ref, x1_ref, p0_ref, p1_ref, a0_ref, a1_ref,
                      m_ref, l_ref, acc_ref, *, out_scale):
    h = pl.program_id(1)
    qi = pl.program_id(2)
    n_kv = vt_ref.shape[0]
    kv_per_q = A_TQ // A_TK
    c = slope_ref[h] * LOG2E

    @pl.when(qi == 0)
    def _():
        rows = lax.broadcasted_iota(jnp.int32, (A_TK, A_TQ), 0)
        cols = lax.broadcasted_iota(jnp.int32, (A_TK, A_TQ), 1)
        cd = c * (rows - cols).astype(F32)
        tbl_ref[0] = cd
        tbl_ref[1] = -cd
        for d in range(kv_per_q):
            tbl_ref[2 + d] = -jnp.abs(cd + c * float(d * A_TK))

    qt_ref[...] = q_ref[...].astype(F32).T.astype(BF16)
    m_ref[...] = jnp.full(m_ref.shape, NEG_BIG, F32)
    l_ref[...] = jnp.zeros(l_ref.shape, F32)
    acc_ref[...] = jnp.zeros(acc_ref.shape, F32)
    p1_ref[...] = jnp.zeros(p1_ref.shape, BF16)
    a1_ref[...] = jnp.ones(a1_ref.shape, F32)

    def bias_of(j):
        d = j - qi * kv_per_q
        off = c * (d * A_TK).astype(F32)
        idx = jnp.where(d < 0, 0, jnp.where(d >= kv_per_q, 1, 2 + d))
        kappa = jnp.where(d < 0, off, jnp.where(d >= kv_per_q, -off, 0.0))
        return idx, kappa

    def biased_logits(j, half, idx):
        k0 = pl.multiple_of(j * A_TK, A_TK)
        cols_h = slice(half * HEAD_DIM, (half + 1) * HEAD_DIM)
        return _dot(k_ref[pl.ds(k0, A_TK), cols_h], qt_ref[cols_h, :]) + tbl_ref[idx]

    bound = scal_ref[1]
    fixed_shift_ok = bound <= A_MAX_FIXED_SHIFT

    @pl.when(fixed_shift_ok)
    def _():
        def probabilities(j, p_ref, live):
            idx, kappa = bias_of(j)
            shift = kappa - bound
            for half in range(2):
                p = jnp.exp2(biased_logits(j, half, idx) + shift)
                l_ref[half] += live * jnp.sum(p, axis=0, keepdims=True)
                p_ref[half] = p.astype(BF16)

        def values(j, p_ref):
            vtb = vt_ref[j]
            for half in range(2):
                acc_ref[half] += _dot(vtb, p_ref[half])

        probabilities(0, p0_ref, 1.0)

        def pair(jj, carry):
            j = 2 * jj
            probabilities(j + 1, p1_ref, 1.0)
            values(j, p0_ref)
            nxt = j + 2
            probabilities(jnp.minimum(nxt, n_kv - 1), p0_ref, (nxt < n_kv).astype(F32))
            values(j + 1, p1_ref)
            return carry

        lax.fori_loop(0, n_kv // 2, pair, 0)

    @pl.when(jnp.logical_not(fixed_shift_ok))
    def _():
        def logits(j, z_ref, zmax_ref):
            idx, _ = bias_of(j)
            for half in range(2):
                z = biased_logits(j, half, idx)
                z_ref[half] = z
                zmax_ref[half] = jnp.max(z, axis=0, keepdims=True)

        def softmax(j, z_ref, zmax_ref, p_ref, a_ref):
            _, kappa = bias_of(j)
            for half in range(2):
                m_old = m_ref[half]
                m_new = jnp.maximum(m_old, zmax_ref[half] + kappa)
                p = jnp.exp2(z_ref[half] - (m_new - kappa))
                alpha = jnp.exp2(m_old - m_new)
                l_ref[half] = alpha * l_ref[half] + jnp.sum(p, axis=0, keepdims=True)
                m_ref[half] = m_new
                a_ref[half] = alpha
                p_ref[half] = p.astype(BF16)

        def values(j, p_ref, a_ref):
            vtb = vt_ref[j]
            for half in range(2):
                acc_ref[half] = a_ref[half] * acc_ref[half] + _dot(vtb, p_ref[half])

        logits(0, s0_ref, x0_ref)

        def pair(jj, carry):
            j = 2 * jj
            logits(j + 1, s1_ref, x1_ref)
            softmax(j, s0_ref, x0_ref, p0_ref, a0_ref)
            values(jnp.maximum(j - 1, 0), p1_ref, a1_ref)
            logits(jnp.minimum(j + 2, n_kv - 1), s0_ref, x0_ref)
            softmax(j + 1, s1_ref, x1_ref, p1_ref, a1_ref)
            values(j, p0_ref, a0_ref)
            return carry

        lax.fori_loop(0, n_kv // 2, pair, 0)
        values(n_kv - 1, p1_ref, a1_ref)

    ot = acc_ref[0] * (1.0 / l_ref[0]) - scal_ref[0] * (acc_ref[1] * (1.0 / l_ref[1]))
    inv_rms = lax.rsqrt(jnp.mean(ot * ot, axis=0, keepdims=True) + RMS_EPS)
    ot = ot * inv_rms * (sg_ref[...] * out_scale)
    o_ref[...] = ot.T.astype(BF16)


def _logit_bound(q_gain, k_gain):
    q_scale = HEAD_DIM ** -0.5 * LOG2E
    return (HEAD_DIM * q_scale * 1.02) * jnp.max(jnp.abs(q_gain.astype(F32))) * jnp.max(jnp.abs(k_gain.astype(F32)))


def _diff_attention(qk, vt, lam, bound, subln_g, batch, seq, lambda_init):
    assert A_TQ % A_TK == 0 and seq % A_TQ == 0
    assert (seq // A_TK) % 2 == 0
    m = qk.shape[0]
    n_q = seq // A_TQ
    n_kv = seq // A_TK
    slopes = jnp.asarray(2.0 ** (-8.0 * np.arange(1, A_HEADS + 1) / A_HEADS), F32)
    kern = functools.partial(_diff_attn_kernel, out_scale=1.0 - lambda_init)
    smem = pl.BlockSpec(memory_space=pltpu.SMEM)
    return pl.pallas_call(
        kern,
        grid=(batch, A_HEADS, n_q),
        in_specs=[
            smem,
            smem,
            pl.BlockSpec((A_TQ, A_VDIM), lambda b, h, i: (b * n_q + i, h)),
            pl.BlockSpec((seq, A_VDIM), lambda b, h, i: (b, A_HEADS + h)),
            pl.BlockSpec((n_kv, A_VDIM, A_TK), lambda b, h, i: (b, h, 0)),
            pl.BlockSpec((A_VDIM, 1), lambda b, h, i: (0, 0)),
        ],
        out_specs=pl.BlockSpec((A_TQ, A_VDIM), lambda b, h, i: (b * n_q + i, h)),
        out_shape=jax.ShapeDtypeStruct((m, A_HEADS * A_VDIM), BF16),
        scratch_shapes=[
            pltpu.VMEM((A_VDIM, A_TQ), BF16),
            pltpu.VMEM((2 + A_TQ // A_TK, A_TK, A_TQ), F32),
            pltpu.VMEM((2, A_TK, A_TQ), F32),
            pltpu.VMEM((2, A_TK, A_TQ), F32),
            pltpu.VMEM((2, 1, A_TQ), F32),
            pltpu.VMEM((2, 1, A_TQ), F32),
            pltpu.VMEM((2, A_TK, A_TQ), BF16),
            pltpu.VMEM((2, A_TK, A_TQ), BF16),
            pltpu.VMEM((2, 1, A_TQ), F32),
            pltpu.VMEM((2, 1, A_TQ), F32),
            pltpu.VMEM((2, 1, A_TQ), F32),
            pltpu.VMEM((2, 1, A_TQ), F32),
            pltpu.VMEM((2, A_VDIM, A_TQ), F32),
        ],
        compiler_params=_params(("parallel", "parallel", "arbitrary")),
        name="diff_attention",
    )(jnp.stack([lam, bound]).astype(F32), slopes, qk, qk, vt, subln_g.reshape(A_VDIM, 1))


B_QROWS = 8
B_KROWS = 16
B_TQ = B_QROWS * GRID_W
B_TK = B_KROWS * GRID_W


def _na_bias_table(rel_bias, rows):
    kr = min(NA_ROWS, rows)
    n_groups = rows // B_QROWS
    n_heads, n_drow, n_dcol = rel_bias.shape
    c = np.arange(GRID_W)
    c_start = np.clip(c - NA_COLS // 2, 0, GRID_W - NA_COLS)
    kc = np.arange(GRID_W)
    col_ok = (kc[None, :] >= c_start[:, None]) & (kc[None, :] < c_start[:, None] + NA_COLS)
    dcol = kc[None, :] - c[:, None] + (NA_COLS - 1)
    select = (np.arange(n_dcol)[:, None, None] == dcol[None]) & col_ok[None]
    cols = jnp.einsum("hrd,dck->hcrk", rel_bias * LOG2E, jnp.asarray(select, F32),
                      precision=lax.Precision.HIGHEST)
    strip = jnp.where(col_ok[None, :, None, :], cols, NEG_BIG).reshape(n_heads, GRID_W, n_drow * GRID_W)
    row_blocks = []
    for g in (0, 1, n_groups - 1):
        k_row0 = int(np.clip(g * B_QROWS - kr // 2, 0, rows - B_KROWS))
        for r in range(g * B_QROWS, (g + 1) * B_QROWS):
            r_start = int(np.clip(r - kr // 2, 0, rows - kr))
            lead = r_start - k_row0
            d0 = r_start - r + (NA_ROWS - 1)
            parts = [jnp.full((n_heads, GRID_W, lead * GRID_W), NEG_BIG, F32),
                     strip[:, :, d0 * GRID_W:(d0 + kr) * GRID_W],
                     jnp.full((n_heads, GRID_W, (B_KROWS - kr - lead) * GRID_W), NEG_BIG, F32)]
            row_blocks.append(jnp.concatenate(parts, axis=-1))
    return jnp.stack(row_blocks, axis=1).reshape(n_heads, 3, B_TQ, B_TK)


def _na_kernel(q_ref, k_ref, v_ref, bias_ref, o_ref, p0_ref, p1_ref, l0_ref, l1_ref, *, rows):
    n_groups = rows // B_QROWS

    def key_start(g):
        k_row0 = jnp.clip(g * B_QROWS - NA_ROWS // 2, 0, rows - B_KROWS)
        return pl.multiple_of(k_row0 * GRID_W, (NA_ROWS // 2) * GRID_W)

    def probabilities(g, p_ref, l_ref):
        q0 = pl.multiple_of(g * B_TQ, B_TQ)
        variant = jnp.where(g == 0, 0, jnp.where(g == n_groups - 1, 2, 1))
        s = _dot_nt(q_ref[pl.ds(q0, B_TQ), :], k_ref[pl.ds(key_start(g), B_TK), :]) + bias_ref[0, variant]
        p = jnp.exp2(s - jnp.max(s, axis=-1, keepdims=True))
        l_ref[...] = jnp.sum(p, axis=-1, keepdims=True)
        p_ref[...] = p.astype(BF16)

    def values(g, p_ref, l_ref):
        q0 = pl.multiple_of(g * B_TQ, B_TQ)
        o = _dot(p_ref[...], v_ref[pl.ds(key_start(g), B_TK), :])
        o_ref[pl.ds(q0, B_TQ), :] = (o * (1.0 / l_ref[...])).astype(BF16)

    probabilities(0, p0_ref, l0_ref)

    def pair(gg, carry):
        g = 2 * gg
        probabilities(g + 1, p1_ref, l1_ref)
        values(g, p0_ref, l0_ref)
        probabilities(jnp.minimum(g + 2, n_groups - 1), p0_ref, l0_ref)
        values(g + 1, p1_ref, l1_ref)
        return carry

    lax.fori_loop(0, n_groups // 2, pair, 0)


def _neighbourhood_attention(qkv, bias, batch, seq):
    m = qkv.shape[0]
    rows = seq // GRID_W
    assert rows % (2 * B_QROWS) == 0 and rows >= B_KROWS and NA_ROWS <= rows
    kern = functools.partial(_na_kernel, rows=rows)
    return pl.pallas_call(
        kern,
        grid=(batch, B_HEADS),
        in_specs=[
            pl.BlockSpec((seq, HEAD_DIM), lambda b, h: (b, h)),
            pl.BlockSpec((seq, HEAD_DIM), lambda b, h: (b, B_HEADS + h)),
            pl.BlockSpec((seq, HEAD_DIM), lambda b, h: (b, 2 * B_HEADS + h)),
            pl.BlockSpec((1, 3, B_TQ, B_TK), lambda b, h: (h, 0, 0, 0)),
        ],
        out_specs=pl.BlockSpec((seq, HEAD_DIM), lambda b, h: (b, h)),
        out_shape=jax.ShapeDtypeStruct((m, B_HEADS * HEAD_DIM), BF16),
        scratch_shapes=[
            pltpu.VMEM((B_TQ, B_TK), BF16),
            pltpu.VMEM((B_TQ, B_TK), BF16),
            pltpu.VMEM((B_TQ, 1), F32),
            pltpu.VMEM((B_TQ, 1), F32),
        ],
        compiler_params=_params(("parallel", "arbitrary")),
        name="neighbourhood_attention",
    )(qkv, qkv, qkv, bias)


C_TQ = 256
C_TK = C_TQ + 2 * C_WINDOW


def _swa_kernel(slope_ref, sink_ref, q_ref, k_ref, v_ref, o_ref, p0_ref, p1_ref, l0_ref, l1_ref):
    kv = pl.program_id(1)
    seq = k_ref.shape[0]
    n_tiles = seq // C_TQ
    rows = lax.broadcasted_iota(jnp.int32, (C_TQ, C_TK), 0)
    cols = lax.broadcasted_iota(jnp.int32, (C_TQ, C_TK), 1)
    delta = rows - cols

    def key_start(t0):
        return pl.multiple_of(jnp.clip(t0 - C_WINDOW, 0, seq - C_TK), C_WINDOW)

    def probabilities(i, p_ref, l_ref):
        t0 = pl.multiple_of(i * C_TQ, C_TQ)
        k0 = key_start(t0)
        dist = jnp.abs(delta + (t0 - k0))
        valid = dist <= C_WINDOW
        dist_f = dist.astype(F32)
        kb = k_ref[pl.ds(k0, C_TK), :]
        for g in range(C_GROUP):
            head = kv * C_GROUP + g
            cols_g = slice(g * HEAD_DIM, (g + 1) * HEAD_DIM)
            s = _dot_nt(q_ref[pl.ds(t0, C_TQ), cols_g], kb)
            s = jnp.where(valid, s - (slope_ref[head] * LOG2E) * dist_f, NEG_BIG)
            sink = sink_ref[head] * LOG2E
            m = jnp.maximum(jnp.max(s, axis=-1, keepdims=True), sink)
            p = jnp.exp2(s - m)
            l_ref[g] = jnp.sum(p, axis=-1, keepdims=True) + jnp.exp2(sink - m)
            p_ref[g] = p.astype(BF16)

    def values(i, p_ref, l_ref):
        t0 = pl.multiple_of(i * C_TQ, C_TQ)
        vb = v_ref[pl.ds(key_start(t0), C_TK), :]
        for g in range(C_GROUP):
            cols_g = slice(g * HEAD_DIM, (g + 1) * HEAD_DIM)
            o_ref[pl.ds(t0, C_TQ), cols_g] = (_dot(p_ref[g], vb) * (1.0 / l_ref[g])).astype(BF16)

    probabilities(0, p0_ref, l0_ref)

    def pair(ii, carry):
        i = 2 * ii
        probabilities(i + 1, p1_ref, l1_ref)
        values(i, p0_ref, l0_ref)
        probabilities(jnp.minimum(i + 2, n_tiles - 1), p0_ref, l0_ref)
        values(i + 1, p1_ref, l1_ref)
        return carry

    lax.fori_loop(0, n_tiles // 2, pair, 0)


def _window_attention(qkv, sink, batch, seq):
    m = qkv.shape[0]
    assert seq % C_TQ == 0 and seq >= C_TK
    slopes = jnp.asarray(2.0 ** (-8.0 * np.arange(1, C_HEADS + 1) / C_HEADS), F32)
    gw = C_GROUP * HEAD_DIM
    smem = pl.BlockSpec(memory_space=pltpu.SMEM)
    return pl.pallas_call(
        _swa_kernel,
        grid=(batch, C_KV_HEADS),
        in_specs=[
            smem,
            smem,
            pl.BlockSpec((seq, gw), lambda b, kv: (b, kv)),
            pl.BlockSpec((seq, HEAD_DIM), lambda b, kv: (b, C_HEADS + kv)),
            pl.BlockSpec((seq, HEAD_DIM), lambda b, kv: (b, C_HEADS + C_KV_HEADS + kv)),
        ],
        out_specs=pl.BlockSpec((seq, gw), lambda b, kv: (b, kv)),
        out_shape=jax.ShapeDtypeStruct((m, C_HEADS * HEAD_DIM), BF16),
        scratch_shapes=[
            pltpu.VMEM((C_GROUP, C_TQ, C_TK), BF16),
            pltpu.VMEM((C_GROUP, C_TQ, C_TK), BF16),
            pltpu.VMEM((C_GROUP, C_TQ, 1), F32),
            pltpu.VMEM((C_GROUP, C_TQ, 1), F32),
        ],
        compiler_params=_params(("parallel", "arbitrary")),
        name="window_attention",
    )(slopes, sink.astype(F32), qkv, qkv, qkv)


def _col_gain(q_gain, k_gain, n_q_heads, n_k_heads, n_cols):
    q_scale = HEAD_DIM ** -0.5 * LOG2E
    parts = [jnp.tile(q_gain.astype(F32) * q_scale, n_q_heads), jnp.tile(k_gain.astype(F32), n_k_heads)]
    gain = jnp.concatenate(parts)
    return jnp.pad(gain, (0, n_cols - gain.shape[0]), constant_values=1.0).reshape(1, n_cols)


def kernel(x, norm_ffn1, ffn1_w_gate, ffn1_w_up, ffn1_w_down, norm_mix, norm_ffn2, ffn2_w_gate, ffn2_w_up, ffn2_w_down, a_w_qkv, a_q_norm, a_k_norm, a_lambda_q1, a_lambda_k1, a_lambda_q2, a_lambda_k2, a_subln, a_w_o, b_w_qkv, b_q_norm, b_k_norm, b_rel_bias, b_w_o, c_w_qkv, c_q_norm, c_k_norm, c_sink, c_w_o):
    batch, seq, d = x.shape
    x = x.reshape(batch * seq, d)
    to_bf16 = lambda w: w.astype(BF16)
    ffn1 = tuple(map(to_bf16, (ffn1_w_gate, ffn1_w_up, ffn1_w_down)))
    ffn2 = tuple(map(to_bf16, (ffn2_w_gate, ffn2_w_up, ffn2_w_down)))
    a_w_qkv, a_w_o, b_w_qkv, b_w_o, c_w_qkv, c_w_o = map(
        to_bf16, (a_w_qkv, a_w_o, b_w_qkv, b_w_o, c_w_qkv, c_w_o))

    for i in range(DEPTH):
        x = _ffn(x, norm_ffn1[i], *ffn1, i)
        kind, j = i % N_MIXERS, i // N_MIXERS
        if kind == 0:
            lambda_init = 0.8 - 0.6 * math.exp(-0.3 * i)
            lam = (jnp.exp(jnp.sum(a_lambda_q1[j].astype(F32) * a_lambda_k1[j].astype(F32)))
                   - jnp.exp(jnp.sum(a_lambda_q2[j].astype(F32) * a_lambda_k2[j].astype(F32))) + lambda_init)
            n_qk = 2 * A_HEADS * 2 * HEAD_DIM
            gain = _col_gain(a_q_norm[j], a_k_norm[j], 2 * A_HEADS, 2 * A_HEADS, n_qk)
            qk = _qkv(x, norm_mix[i], a_w_qkv, j, gain, n_qk, n=n_qk, sub_blocks=QKV_SUB_BLOCKS)
            vt = _proj_t(x, norm_mix[i], a_w_qkv, j, n_qk)
            bound = _logit_bound(a_q_norm[j], a_k_norm[j])
            o = _diff_attention(qk, vt, lam, bound, a_subln[j], batch, seq, lambda_init)
            x = _out_proj(o, a_w_o, j, x)
        elif kind == 1:
            n_qk = 2 * B_HEADS * HEAD_DIM
            gain = _col_gain(b_q_norm[j], b_k_norm[j], B_HEADS, B_HEADS, b_w_qkv.shape[2])
            qkv = _qkv(x, norm_mix[i], b_w_qkv, j, gain, n_qk, sub_blocks=QKV_SUB_BLOCKS)
            bias = _na_bias_table(b_rel_bias[j].astype(F32), seq // GRID_W)
            o = _neighbourhood_attention(qkv, bias, batch, seq)
            x = _out_proj(o, b_w_o, j, x)
        else:
            n_qk = (C_HEADS + C_KV_HEADS) * HEAD_DIM
            gain = _col_gain(c_q_norm[j], c_k_norm[j], C_HEADS, C_KV_HEADS, c_w_qkv.shape[2])
            qkv = _qkv(x, norm_mix[i], c_w_qkv, j, gain, n_qk)
            o = _window_attention(qkv, c_sink[j], batch, seq)
            x = _out_proj(o, c_w_o, j, x)
        x = _ffn(x, norm_ffn2[i], *ffn2, i)
    return x.reshape(batch, seq, d)
```

```python
import functools
import math

import jax
import jax.numpy as jnp
import numpy as np
from jax import lax
from jax.experimental import pallas as pl
from jax.experimental.pallas import tpu as pltpu

D_MODEL = 2048
DEPTH = 4
N_MIXERS = 3
HEAD_DIM = 128
D_FF = 5632
RMS_EPS = 1e-6
A_HEADS = D_MODEL // (2 * HEAD_DIM)
A_VDIM = 2 * HEAD_DIM
B_HEADS = D_MODEL // HEAD_DIM
GRID_W = 64
NA_ROWS = 8
NA_COLS = 16
C_HEADS = D_MODEL // HEAD_DIM
C_KV_HEADS = 4
C_GROUP = C_HEADS // C_KV_HEADS
C_WINDOW = 128

LOG2E = math.log2(math.e)
NEG_BIG = -1e30
BF16 = jnp.bfloat16
F32 = jnp.float32

V7X_VMEM_BYTES = 64 * 1024 * 1024
VMEM_LIMIT = 56 * 1024 * 1024
LANE = 128
SUBLANE = 8


def _params(semantics):
    return pltpu.CompilerParams(dimension_semantics=semantics, vmem_limit_bytes=VMEM_LIMIT)


def _dot(a, b):
    return jnp.dot(a, b, preferred_element_type=F32)


def _dot_nt(a, b):
    return lax.dot_general(a, b, (((1,), (1,)), ((), ())), preferred_element_type=F32)


def _rms_normalise(x):
    return x * lax.rsqrt(jnp.mean(x * x, axis=-1, keepdims=True) + RMS_EPS)


FFN_TM = 1024
FFN_TF = 512


def _ffn_kernel(x_ref, g_ref, wg_ref, wu_ref, wd_ref, o_ref, h_ref):
    j = pl.program_id(1)

    @pl.when(j == 0)
    def _():
        x = x_ref[...]
        h_ref[...] = (_rms_normalise(x) * g_ref[...]).astype(BF16)
        o_ref[...] = x

    h = h_ref[...]
    gate = _dot(h, wg_ref[...])
    up = _dot(h, wu_ref[...])
    act = gate * (0.5 / (1.0 + jnp.exp(-gate))) * up
    o_ref[...] += _dot(act.astype(BF16), wd_ref[...])


def _ffn(x, g, wg, wu, wd, layer):
    m, d = x.shape
    f = wg.shape[2]
    return pl.pallas_call(
        _ffn_kernel,
        grid=(m // FFN_TM, f // FFN_TF),
        in_specs=[
            pl.BlockSpec((FFN_TM, d), lambda i, j: (i, 0)),
            pl.BlockSpec((1, d), lambda i, j: (0, 0)),
            pl.BlockSpec((None, d, FFN_TF), lambda i, j: (layer, 0, j)),
            pl.BlockSpec((None, d, FFN_TF), lambda i, j: (layer, 0, j)),
            pl.BlockSpec((None, FFN_TF, d), lambda i, j: (layer, j, 0)),
        ],
        out_specs=pl.BlockSpec((FFN_TM, d), lambda i, j: (i, 0)),
        out_shape=jax.ShapeDtypeStruct((m, d), F32),
        scratch_shapes=[pltpu.VMEM((FFN_TM, d), BF16)],
        compiler_params=_params(("parallel", "arbitrary")),
        name="macaron_ffn",
    )(x, g.reshape(1, d), wg, wu, wd)


PROJ_TM = 1024
PROJ_TN = 512
QKV_SUB_BLOCKS = 4


def _qkv_kernel(x_ref, g_ref, w_ref, cg_ref, o_ref, h_ref, *, n_norm_steps, sub_blocks):
    j = pl.program_id(1)

    @pl.when(j == 0)
    def _():
        h_ref[...] = (_rms_normalise(x_ref[...]) * g_ref[...]).astype(BF16)

    def project(blk):
        cols = slice(blk * PROJ_TN, (blk + 1) * PROJ_TN)
        return _dot(h_ref[...], w_ref[:, cols])

    @pl.when(j < n_norm_steps)
    def _():
        for blk in range(sub_blocks):
            y = project(blk)
            for c in range(PROJ_TN // HEAD_DIM):
                cols = slice(blk * PROJ_TN + c * HEAD_DIM, blk * PROJ_TN + (c + 1) * HEAD_DIM)
                head = slice(c * HEAD_DIM, (c + 1) * HEAD_DIM)
                o_ref[:, cols] = (_rms_normalise(y[:, head]) * cg_ref[:, cols]).astype(BF16)

    @pl.when(j >= n_norm_steps)
    def _():
        for blk in range(sub_blocks):
            o_ref[:, blk * PROJ_TN:(blk + 1) * PROJ_TN] = project(blk).astype(BF16)


def _qkv(x, g, w, layer, col_gain, n_norm_cols, n=None, sub_blocks=1):
    m, d = x.shape
    n = w.shape[2] if n is None else n
    tn = sub_blocks * PROJ_TN
    assert n % tn == 0 and n_norm_cols % tn == 0
    kern = functools.partial(_qkv_kernel, n_norm_steps=n_norm_cols // tn, sub_blocks=sub_blocks)
    return pl.pallas_call(
        kern,
        grid=(m // PROJ_TM, n // tn),
        in_specs=[
            pl.BlockSpec((PROJ_TM, d), lambda i, j: (i, 0)),
            pl.BlockSpec((1, d), lambda i, j: (0, 0)),
            pl.BlockSpec((None, d, tn), lambda i, j: (layer, 0, j)),
            pl.BlockSpec((1, tn), lambda i, j: (0, j)),
        ],
        out_specs=pl.BlockSpec((PROJ_TM, tn), lambda i, j: (i, j)),
        out_shape=jax.ShapeDtypeStruct((m, n), BF16),
        scratch_shapes=[pltpu.VMEM((PROJ_TM, d), BF16)],
        compiler_params=_params(("parallel", "arbitrary")),
        name="mixer_qkv",
    )(x, g.reshape(1, d), w, col_gain)


A_TK = 512


def _proj_t_kernel(x_ref, g_ref, w_ref, o_ref, h_ref):
    @pl.when(pl.program_id(1) == 0)
    def _():
        h_ref[...] = (_rms_normalise(x_ref[...]) * g_ref[...]).astype(BF16)

    yt = _dot(h_ref[...], w_ref[...]).T.astype(BF16)
    for r in range(PROJ_TM // A_TK):
        o_ref[r] = yt[:, r * A_TK:(r + 1) * A_TK]


def _proj_t(x, g, w, layer, col0):
    m, d = x.shape
    n = w.shape[2] - col0
    j0 = col0 // PROJ_TN
    return pl.pallas_call(
        _proj_t_kernel,
        grid=(m // PROJ_TM, n // PROJ_TN),
        in_specs=[
            pl.BlockSpec((PROJ_TM, d), lambda i, j: (i, 0)),
            pl.BlockSpec((1, d), lambda i, j: (0, 0)),
            pl.BlockSpec((None, d, PROJ_TN), lambda i, j: (layer, 0, j0 + j)),
        ],
        out_specs=pl.BlockSpec((PROJ_TM // A_TK, PROJ_TN, A_TK), lambda i, j: (i, j, 0)),
        out_shape=jax.ShapeDtypeStruct((m // A_TK, n, A_TK), BF16),
        scratch_shapes=[pltpu.VMEM((PROJ_TM, d), BF16)],
        compiler_params=_params(("parallel", "arbitrary")),
        name="mixer_v_transposed",
    )(x, g.reshape(1, d), w)


def _out_proj_kernel(a_ref, w_ref, x_ref, o_ref):
    o_ref[...] = x_ref[...] + _dot(a_ref[...], w_ref[...])


OUT_TM = 512


def _out_proj(a, w, layer, x):
    m, k = a.shape
    n = w.shape[2]
    return pl.pallas_call(
        _out_proj_kernel,
        grid=(m // OUT_TM,),
        in_specs=[
            pl.BlockSpec((OUT_TM, k), lambda i: (i, 0)),
            pl.BlockSpec((None, k, n), lambda i: (layer, 0, 0)),
            pl.BlockSpec((OUT_TM, n), lambda i: (i, 0)),
        ],
        out_specs=pl.BlockSpec((OUT_TM, n), lambda i: (i, 0)),
        out_shape=jax.ShapeDtypeStruct((m, n), F32),
        compiler_params=_params(("parallel",)),
        name="mixer_out_proj",
    )(a, w, x)


A_TQ = 512
A_MAX_FIXED_SHIFT = 40.0


def _diff_attn_kernel(scal_ref, slope_ref, q_ref, k_ref, vt_ref, sg_ref, *refs, out_scale, n_cast):
    cast_src, o_ref, cast_dst = refs[:n_cast], refs[n_cast], refs[n_cast + 1:2 * n_cast + 1]
    (qt_ref, tbl_ref, s0_ref, s1_ref, x0_ref, x1_ref, p0_ref, p1_ref, a0_ref, a1_ref,
     m_ref, l_ref, acc_ref) = refs[2 * n_cast + 1:]
    for src_ref, dst_ref in zip(cast_src, cast_dst):
        dst_ref[...] = src_ref[...].astype(BF16)

    h = pl.program_id(1)
    qi = pl.program_id(2)
    n_kv = vt_ref.shape[0]
    kv_per_q = A_TQ // A_TK
    c = slope_ref[h] * LOG2E

    @pl.when(qi == 0)
    def _():
        rows = lax.broadcasted_iota(jnp.int32, (A_TK, A_TQ), 0)
        cols = lax.broadcasted_iota(jnp.int32, (A_TK, A_TQ), 1)
        cd = c * (rows - cols).astype(F32)
        tbl_ref[0] = cd
        tbl_ref[1] = -cd
        for d in range(kv_per_q):
            tbl_ref[2 + d] = -jnp.abs(cd + c * float(d * A_TK))

    qt_ref[...] = q_ref[...].astype(F32).T.astype(BF16)
    m_ref[...] = jnp.full(m_ref.shape, NEG_BIG, F32)
    l_ref[...] = jnp.zeros(l_ref.shape, F32)
    acc_ref[...] = jnp.zeros(acc_ref.shape, F32)
    p1_ref[...] = jnp.zeros(p1_ref.shape, BF16)
    a1_ref[...] = jnp.ones(a1_ref.shape, F32)

    def bias_of(j):
        d = j - qi * kv_per_q
        off = c * (d * A_TK).astype(F32)
        idx = jnp.where(d < 0, 0, jnp.where(d >= kv_per_q, 1, 2 + d))
        kappa = jnp.where(d < 0, off, jnp.where(d >= kv_per_q, -off, 0.0))
        return idx, kappa

    def biased_logits(j, half, idx):
        k0 = pl.multiple_of(j * A_TK, A_TK)
        cols_h = slice(half * HEAD_DIM, (half + 1) * HEAD_DIM)
        return _dot(k_ref[pl.ds(k0, A_TK), cols_h], qt_ref[cols_h, :]) + tbl_ref[idx]

    bound = scal_ref[1]
    fixed_shift_ok = bound <= A_MAX_FIXED_SHIFT

    @pl.when(fixed_shift_ok)
    def _():
        def probabilities(j, p_ref, live):
            idx, kappa = bias_of(j)
            shift = kappa - bound
            for half in range(2):
                p = jnp.exp2(biased_logits(j, half, idx) + shift)
                l_ref[half] += live * jnp.sum(p, axis=0, keepdims=True)
                p_ref[half] = p.astype(BF16)

        def values(j, p_ref):
            vtb = vt_ref[j]
            for half in range(2):
                acc_ref[half] += _dot(vtb, p_ref[half])

        probabilities(0, p0_ref, 1.0)

        def pair(jj, carry):
            j = 2 * jj
            probabilities(j + 1, p1_ref, 1.0)
            values(j, p0_ref)
            nxt = j + 2
            probabilities(jnp.minimum(nxt, n_kv - 1), p0_ref, (nxt < n_kv).astype(F32))
            values(j + 1, p1_ref)
            return carry

        lax.fori_loop(0, n_kv // 2, pair, 0)

    @pl.when(jnp.logical_not(fixed_shift_ok))
    def _():
        def logits(j, z_ref, zmax_ref):
            idx, _ = bias_of(j)
            for half in range(2):
                z = biased_logits(j, half, idx)
                z_ref[half] = z
                zmax_ref[half] = jnp.max(z, axis=0, keepdims=True)

        def softmax(j, z_ref, zmax_ref, p_ref, a_ref):
            _, kappa = bias_of(j)
            for half in range(2):
                m_old = m_ref[half]
                m_new = jnp.maximum(m_old, zmax_ref[half] + kappa)
                p = jnp.exp2(z_ref[half] - (m_new - kappa))
                alpha = jnp.exp2(m_old - m_new)
                l_ref[half] = alpha * l_ref[half] + jnp.sum(p, axis=0, keepdims=True)
                m_ref[half] = m_new
                a_ref[half] = alpha
                p_ref[half] = p.astype(BF16)

        def values(j, p_ref, a_ref):
            vtb = vt_ref[j]
            for half in range(2):
                acc_ref[half] = a_ref[half] * acc_ref[half] + _dot(vtb, p_ref[half])

        logits(0, s0_ref, x0_ref)

        def pair(jj, carry):
            j = 2 * jj
            logits(j + 1, s1_ref, x1_ref)
            softmax(j, s0_ref, x0_ref, p0_ref, a0_ref)
            values(jnp.maximum(j - 1, 0), p1_ref, a1_ref)
            logits(jnp.minimum(j + 2, n_kv - 1), s0_ref, x0_ref)
            softmax(j + 1, s1_ref, x1_ref, p1_ref, a1_ref)
            values(j, p0_ref, a0_ref)
            return carry

        lax.fori_loop(0, n_kv // 2, pair, 0)
        values(n_kv - 1, p1_ref, a1_ref)

    ot = acc_ref[0] * (1.0 / l_ref[0]) - scal_ref[0] * (acc_ref[1] * (1.0 / l_ref[1]))
    inv_rms = lax.rsqrt(jnp.mean(ot * ot, axis=0, keepdims=True) + RMS_EPS)
    ot = ot * inv_rms * (sg_ref[...] * out_scale)
    o_ref[...] = ot.T.astype(BF16)


def _logit_bound(q_gain, k_gain):
    q_scale = HEAD_DIM ** -0.5 * LOG2E
    return (HEAD_DIM * q_scale * 1.02) * jnp.max(jnp.abs(q_gain.astype(F32))) * jnp.max(jnp.abs(k_gain.astype(F32)))


def _cast_tiling(n_elems, n_steps):
    per_step = n_elems // n_steps
    rows = 2 * SUBLANE
    assert n_elems == per_step * n_steps and per_step % (rows * LANE) == 0
    while rows < 64 and per_step % (2 * rows * LANE) == 0:
        rows *= 2
    return rows, per_step // rows


def _diff_attention(qk, vt, lam, bound, subln_g, batch, seq, lambda_init, cast=()):
    assert A_TQ % A_TK == 0 and seq % A_TQ == 0
    assert (seq // A_TK) % 2 == 0
    m = qk.shape[0]
    n_q = seq // A_TQ
    n_kv = seq // A_TK
    n_steps = batch * A_HEADS * n_q
    slopes = jnp.asarray(2.0 ** (-8.0 * np.arange(1, A_HEADS + 1) / A_HEADS), F32)
    kern = functools.partial(_diff_attn_kernel, out_scale=1.0 - lambda_init, n_cast=len(cast))
    smem = pl.BlockSpec(memory_space=pltpu.SMEM)
    step_of = lambda b, h, i: ((b * A_HEADS + h) * n_q + i, 0)
    tilings = [_cast_tiling(w.size, n_steps) for w in cast]
    cast_specs = [pl.BlockSpec(t, step_of) for t in tilings]
    cast_views = [w.reshape(n_steps * r, c) for w, (r, c) in zip(cast, tilings)]
    out = pl.pallas_call(
        kern,
        grid=(batch, A_HEADS, n_q),
        in_specs=[
            smem,
            smem,
            pl.BlockSpec((A_TQ, A_VDIM), lambda b, h, i: (b * n_q + i, h)),
            pl.BlockSpec((seq, A_VDIM), lambda b, h, i: (b, A_HEADS + h)),
            pl.BlockSpec((n_kv, A_VDIM, A_TK), lambda b, h, i: (b, h, 0)),
            pl.BlockSpec((A_VDIM, 1), lambda b, h, i: (0, 0)),
        ] + cast_specs,
        out_specs=[pl.BlockSpec((A_TQ, A_VDIM), lambda b, h, i: (b * n_q + i, h))] + cast_specs,
        out_shape=[jax.ShapeDtypeStruct((m, A_HEADS * A_VDIM), BF16)]
        + [jax.ShapeDtypeStruct(v.shape, BF16) for v in cast_views],
        scratch_shapes=[
            pltpu.VMEM((A_VDIM, A_TQ), BF16),
            pltpu.VMEM((2 + A_TQ // A_TK, A_TK, A_TQ), F32),
            pltpu.VMEM((2, A_TK, A_TQ), F32),
            pltpu.VMEM((2, A_TK, A_TQ), F32),
            pltpu.VMEM((2, 1, A_TQ), F32),
            pltpu.VMEM((2, 1, A_TQ), F32),
            pltpu.VMEM((2, A_TK, A_TQ), BF16),
            pltpu.VMEM((2, A_TK, A_TQ), BF16),
            pltpu.VMEM((2, 1, A_TQ), F32),
            pltpu.VMEM((2, 1, A_TQ), F32),
            pltpu.VMEM((2, 1, A_TQ), F32),
            pltpu.VMEM((2, 1, A_TQ), F32),
            pltpu.VMEM((2, A_VDIM, A_TQ), F32),
        ],
        compiler_params=_params(("parallel", "parallel", "arbitrary")),
        name="diff_attention",
    )(jnp.stack([lam, bound]).astype(F32), slopes, qk, qk, vt, subln_g.reshape(A_VDIM, 1), *cast_views)
    return out[0], tuple(c.reshape(w.shape) for c, w in zip(out[1:], cast))


B_QROWS = 8
B_KROWS = 16
B_TQ = B_QROWS * GRID_W
B_TK = B_KROWS * GRID_W


def _na_bias_table(rel_bias, rows):
    kr = min(NA_ROWS, rows)
    n_groups = rows // B_QROWS
    n_heads, n_drow, n_dcol = rel_bias.shape
    c = np.arange(GRID_W)
    c_start = np.clip(c - NA_COLS // 2, 0, GRID_W - NA_COLS)
    kc = np.arange(GRID_W)
    col_ok = (kc[None, :] >= c_start[:, None]) & (kc[None, :] < c_start[:, None] + NA_COLS)
    dcol = kc[None, :] - c[:, None] + (NA_COLS - 1)
    select = (np.arange(n_dcol)[:, None, None] == dcol[None]) & col_ok[None]
    cols = jnp.einsum("hrd,dck->hcrk", rel_bias * LOG2E, jnp.asarray(select, F32),
                      precision=lax.Precision.HIGHEST)
    strip = jnp.where(col_ok[None, :, None, :], cols, NEG_BIG).reshape(n_heads, GRID_W, n_drow * GRID_W)
    row_blocks = []
    for g in (0, 1, n_groups - 1):
        k_row0 = int(np.clip(g * B_QROWS - kr // 2, 0, rows - B_KROWS))
        for r in range(g * B_QROWS, (g + 1) * B_QROWS):
            r_start = int(np.clip(r - kr // 2, 0, rows - kr))
            lead = r_start - k_row0
            d0 = r_start - r + (NA_ROWS - 1)
            parts = [jnp.full((n_heads, GRID_W, lead * GRID_W), NEG_BIG, F32),
                     strip[:, :, d0 * GRID_W:(d0 + kr) * GRID_W],
                     jnp.full((n_heads, GRID_W, (B_KROWS - kr - lead) * GRID_W), NEG_BIG, F32)]
            row_blocks.append(jnp.concatenate(parts, axis=-1))
    return jnp.stack(row_blocks, axis=1).reshape(n_heads, 3, B_TQ, B_TK)


def _na_kernel(q_ref, k_ref, v_ref, bias_ref, o_ref, p0_ref, p1_ref, l0_ref, l1_ref, *, rows):
    n_groups = rows // B_QROWS

    def key_start(g):
        k_row0 = jnp.clip(g * B_QROWS - NA_ROWS // 2, 0, rows - B_KROWS)
        return pl.multiple_of(k_row0 * GRID_W, (NA_ROWS // 2) * GRID_W)

    def probabilities(g, p_ref, l_ref):
        q0 = pl.multiple_of(g * B_TQ, B_TQ)
        variant = jnp.where(g == 0, 0, jnp.where(g == n_groups - 1, 2, 1))
        s = _dot_nt(q_ref[pl.ds(q0, B_TQ), :], k_ref[pl.ds(key_start(g), B_TK), :]) + bias_ref[0, variant]
        p = jnp.exp2(s - jnp.max(s, axis=-1, keepdims=True))
        l_ref[...] = jnp.sum(p, axis=-1, keepdims=True)
        p_ref[...] = p.astype(BF16)

    def values(g, p_ref, l_ref):
        q0 = pl.multiple_of(g * B_TQ, B_TQ)
        o = _dot(p_ref[...], v_ref[pl.ds(key_start(g), B_TK), :])
        o_ref[pl.ds(q0, B_TQ), :] = (o * (1.0 / l_ref[...])).astype(BF16)

    probabilities(0, p0_ref, l0_ref)

    def pair(gg, carry):
        g = 2 * gg
        probabilities(g + 1, p1_ref, l1_ref)
        values(g, p0_ref, l0_ref)
        probabilities(jnp.minimum(g + 2, n_groups - 1), p0_ref, l0_ref)
        values(g + 1, p1_ref, l1_ref)
        return carry

    lax.fori_loop(0, n_groups // 2, pair, 0)


def _neighbourhood_attention(qkv, bias, batch, seq):
    m = qkv.shape[0]
    rows = seq // GRID_W
    assert rows % (2 * B_QROWS) == 0 and rows >= B_KROWS and NA_ROWS <= rows
    kern = functools.partial(_na_kernel, rows=rows)
    return pl.pallas_call(
        kern,
        grid=(batch, B_HEADS),
        in_specs=[
            pl.BlockSpec((seq, HEAD_DIM), lambda b, h: (b, h)),
            pl.BlockSpec((seq, HEAD_DIM), lambda b, h: (b, B_HEADS + h)),
            pl.BlockSpec((seq, HEAD_DIM), lambda b, h: (b, 2 * B_HEADS + h)),
            pl.BlockSpec((1, 3, B_TQ, B_TK), lambda b, h: (h, 0, 0, 0)),
        ],
        out_specs=pl.BlockSpec((seq, HEAD_DIM), lambda b, h: (b, h)),
        out_shape=jax.ShapeDtypeStruct((m, B_HEADS * HEAD_DIM), BF16),
        scratch_shapes=[
            pltpu.VMEM((B_TQ, B_TK), BF16),
            pltpu.VMEM((B_TQ, B_TK), BF16),
            pltpu.VMEM((B_TQ, 1), F32),
            pltpu.VMEM((B_TQ, 1), F32),
        ],
        compiler_params=_params(("parallel", "arbitrary")),
        name="neighbourhood_attention",
    )(qkv, qkv, qkv, bias)


C_TQ = 256
C_TK = C_TQ + 2 * C_WINDOW


def _swa_kernel(slope_ref, sink_ref, q_ref, k_ref, v_ref, o_ref, p0_ref, p1_ref, l0_ref, l1_ref):
    kv = pl.program_id(1)
    seq = k_ref.shape[0]
    n_tiles = seq // C_TQ
    rows = lax.broadcasted_iota(jnp.int32, (C_TQ, C_TK), 0)
    cols = lax.broadcasted_iota(jnp.int32, (C_TQ, C_TK), 1)
    delta = rows - cols

    def key_start(t0):
        return pl.multiple_of(jnp.clip(t0 - C_WINDOW, 0, seq - C_TK), C_WINDOW)

    def probabilities(i, p_ref, l_ref):
        t0 = pl.multiple_of(i * C_TQ, C_TQ)
        k0 = key_start(t0)
        dist = jnp.abs(delta + (t0 - k0))
        valid = dist <= C_WINDOW
        dist_f = dist.astype(F32)
        kb = k_ref[pl.ds(k0, C_TK), :]
        for g in range(C_GROUP):
            head = kv * C_GROUP + g
            cols_g = slice(g * HEAD_DIM, (g + 1) * HEAD_DIM)
            s = _dot_nt(q_ref[pl.ds(t0, C_TQ), cols_g], kb)
            s = jnp.where(valid, s - (slope_ref[head] * LOG2E) * dist_f, NEG_BIG)
            sink = sink_ref[head] * LOG2E
            m = jnp.maximum(jnp.max(s, axis=-1, keepdims=True), sink)
            p = jnp.exp2(s - m)
            l_ref[g] = jnp.sum(p, axis=-1, keepdims=True) + jnp.exp2(sink - m)
            p_ref[g] = p.astype(BF16)

    def values(i, p_ref, l_ref):
        t0 = pl.multiple_of(i * C_TQ, C_TQ)
        vb = v_ref[pl.ds(key_start(t0), C_TK), :]
        for g in range(C_GROUP):
            cols_g = slice(g * HEAD_DIM, (g + 1) * HEAD_DIM)
            o_ref[pl.ds(t0, C_TQ), cols_g] = (_dot(p_ref[g], vb) * (1.0 / l_ref[g])).astype(BF16)

    probabilities(0, p0_ref, l0_ref)

    def pair(ii, carry):
        i = 2 * ii
        probabilities(i + 1, p1_ref, l1_ref)
        values(i, p0_ref, l0_ref)
        probabilities(jnp.minimum(i + 2, n_tiles - 1), p0_ref, l0_ref)
        values(i + 1, p1_ref, l1_ref)
        return carry

    lax.fori_loop(0, n_tiles // 2, pair, 0)


def _window_attention(qkv, sink, batch, seq):
    m = qkv.shape[0]
    assert seq % C_TQ == 0 and seq >= C_TK
    slopes = jnp.asarray(2.0 ** (-8.0 * np.arange(1, C_HEADS + 1) / C_HEADS), F32)
    gw = C_GROUP * HEAD_DIM
    smem = pl.BlockSpec(memory_space=pltpu.SMEM)
    return pl.pallas_call(
        _swa_kernel,
        grid=(batch, C_KV_HEADS),
        in_specs=[
            smem,
            smem,
            pl.BlockSpec((seq, gw), lambda b, kv: (b, kv)),
            pl.BlockSpec((seq, HEAD_DIM), lambda b, kv: (b, C_HEADS + kv)),
            pl.BlockSpec((seq, HEAD_DIM), lambda b, kv: (b, C_HEADS + C_KV_HEADS + kv)),
        ],
        out_specs=pl.BlockSpec((seq, gw), lambda b, kv: (b, kv)),
        out_shape=jax.ShapeDtypeStruct((m, C_HEADS * HEAD_DIM), BF16),
        scratch_shapes=[
            pltpu.VMEM((C_GROUP, C_TQ, C_TK), BF16),
            pltpu.VMEM((C_GROUP, C_TQ, C_TK), BF16),
            pltpu.VMEM((C_GROUP, C_TQ, 1), F32),
            pltpu.VMEM((C_GROUP, C_TQ, 1), F32),
        ],
        compiler_params=_params(("parallel", "arbitrary")),
        name="window_attention",
    )(slopes, sink.astype(F32), qkv, qkv, qkv)


def _col_gain(q_gain, k_gain, n_q_heads, n_k_heads, n_cols):
    q_scale = HEAD_DIM ** -0.5 * LOG2E
    parts = [jnp.tile(q_gain.astype(F32) * q_scale, n_q_heads), jnp.tile(k_gain.astype(F32), n_k_heads)]
    gain = jnp.concatenate(parts)
    return jnp.pad(gain, (0, n_cols - gain.shape[0]), constant_values=1.0).reshape(1, n_cols)


def kernel(x, norm_ffn1, ffn1_w_gate, ffn1_w_up, ffn1_w_down, norm_mix, norm_ffn2, ffn2_w_gate, ffn2_w_up, ffn2_w_down, a_w_qkv, a_q_norm, a_k_norm, a_lambda_q1, a_lambda_k1, a_lambda_q2, a_lambda_k2, a_subln, a_w_o, b_w_qkv, b_q_norm, b_k_norm, b_rel_bias, b_w_o, c_w_qkv, c_q_norm, c_k_norm, c_sink, c_w_o):
    batch, seq, d = x.shape
    x = x.reshape(batch * seq, d)
    first_slab = lambda w: w[:1].astype(BF16)
    ffn1 = tuple(map(first_slab, (ffn1_w_gate, ffn1_w_up, ffn1_w_down)))
    ffn2 = None
    later_f32 = (ffn1_w_gate, ffn1_w_up, ffn1_w_down, ffn2_w_gate, ffn2_w_up, ffn2_w_down,
                 a_w_qkv, a_w_o, b_w_qkv, b_w_o, c_w_qkv, c_w_o)
    a_w_qkv, a_w_o = first_slab(a_w_qkv), first_slab(a_w_o)

    for i in range(DEPTH):
        x = _ffn(x, norm_ffn1[i], *ffn1, i)
        kind, j = i % N_MIXERS, i // N_MIXERS
        if kind == 0:
            lambda_init = 0.8 - 0.6 * math.exp(-0.3 * i)
            lam = (jnp.exp(jnp.sum(a_lambda_q1[j].astype(F32) * a_lambda_k1[j].astype(F32)))
                   - jnp.exp(jnp.sum(a_lambda_q2[j].astype(F32) * a_lambda_k2[j].astype(F32))) + lambda_init)
            n_qk = 2 * A_HEADS * 2 * HEAD_DIM
            gain = _col_gain(a_q_norm[j], a_k_norm[j], 2 * A_HEADS, 2 * A_HEADS, n_qk)
            qk = _qkv(x, norm_mix[i], a_w_qkv, j, gain, n_qk, n=n_qk, sub_blocks=QKV_SUB_BLOCKS)
            vt = _proj_t(x, norm_mix[i], a_w_qkv, j, n_qk)
            bound = _logit_bound(a_q_norm[j], a_k_norm[j])
            o, cast = _diff_attention(qk, vt, lam, bound, a_subln[j], batch, seq, lambda_init,
                                      cast=later_f32 if i == 0 else ())
            x = _out_proj(o, a_w_o, j, x)
            if i == 0:
                ffn1, ffn2 = cast[:3], cast[3:6]
                a_w_qkv, a_w_o, b_w_qkv, b_w_o, c_w_qkv, c_w_o = cast[6:]
        elif kind == 1:
            n_qk = 2 * B_HEADS * HEAD_DIM
            gain = _col_gain(b_q_norm[j], b_k_norm[j], B_HEADS, B_HEADS, b_w_qkv.shape[2])
            qkv = _qkv(x, norm_mix[i], b_w_qkv, j, gain, n_qk, sub_blocks=QKV_SUB_BLOCKS)
            bias = _na_bias_table(b_rel_bias[j].astype(F32), seq // GRID_W)
            o = _neighbourhood_attention(qkv, bias, batch, seq)
            x = _out_proj(o, b_w_o, j, x)
        else:
            n_qk = (C_HEADS + C_KV_HEADS) * HEAD_DIM
            gain = _col_gain(c_q_norm[j], c_k_norm[j], C_HEADS, C_KV_HEADS, c_w_qkv.shape[2])
            qkv = _qkv(x, norm_mix[i], c_w_qkv, j, gain, n_qk)
            o = _window_attention(qkv, c_sink[j], batch, seq)
            x = _out_proj(o, c_w_o, j, x)
        x = _ffn(x, norm_ffn2[i], *ffn2, i)
    return x.reshape(batch, seq, d)
```

```python
import functools
import math

import jax
import jax.numpy as jnp
import numpy as np
from jax import lax
from jax.experimental import pallas as pl
from jax.experimental.pallas import tpu as pltpu

D_MODEL = 2048
DEPTH = 4
N_MIXERS = 3
HEAD_DIM = 128
D_FF = 5632
RMS_EPS = 1e-6
A_HEADS = D_MODEL // (2 * HEAD_DIM)
A_VDIM = 2 * HEAD_DIM
B_HEADS = D_MODEL // HEAD_DIM
GRID_W = 64
NA_ROWS = 8
NA_COLS = 16
C_HEADS = D_MODEL // HEAD_DIM
C_KV_HEADS = 4
C_GROUP = C_HEADS // C_KV_HEADS
C_WINDOW = 128

LOG2E = math.log2(math.e)
NEG_BIG = -1e30
BF16 = jnp.bfloat16
F32 = jnp.float32

V7X_VMEM_BYTES = 64 * 1024 * 1024
VMEM_LIMIT = 56 * 1024 * 1024
LANE = 128
SUBLANE = 8


def _params(semantics):
    return pltpu.CompilerParams(dimension_semantics=semantics, vmem_limit_bytes=VMEM_LIMIT)


def _dot(a, b):
    return jnp.dot(a, b, preferred_element_type=F32)


def _dot_nt(a, b):
    return lax.dot_general(a, b, (((1,), (1,)), ((), ())), preferred_element_type=F32)


def _rms_normalise(x):
    return x * lax.rsqrt(jnp.mean(x * x, axis=-1, keepdims=True) + RMS_EPS)


FFN_TM = 1024
FFN_TF = 512


def _ffn_kernel(x_ref, g_ref, wg_ref, wu_ref, wd_ref, o_ref, h_ref):
    j = pl.program_id(1)

    @pl.when(j == 0)
    def _():
        x = x_ref[...]
        h_ref[...] = (_rms_normalise(x) * g_ref[...]).astype(BF16)
        o_ref[...] = x

    h = h_ref[...]
    gate = _dot(h, wg_ref[...])
    up = _dot(h, wu_ref[...])
    act = gate * (0.5 / (1.0 + jnp.exp(-gate))) * up
    o_ref[...] += _dot(act.astype(BF16), wd_ref[...])


def _ffn(x, g, wg, wu, wd, layer):
    m, d = x.shape
    f = wg.shape[2]
    return pl.pallas_call(
        _ffn_kernel,
        grid=(m // FFN_TM, f // FFN_TF),
        in_specs=[
            pl.BlockSpec((FFN_TM, d), lambda i, j: (i, 0)),
            pl.BlockSpec((1, d), lambda i, j: (0, 0)),
            pl.BlockSpec((None, d, FFN_TF), lambda i, j: (layer, 0, j)),
            pl.BlockSpec((None, d, FFN_TF), lambda i, j: (layer, 0, j)),
            pl.BlockSpec((None, FFN_TF, d), lambda i, j: (layer, j, 0)),
        ],
        out_specs=pl.BlockSpec((FFN_TM, d), lambda i, j: (i, 0)),
        out_shape=jax.ShapeDtypeStruct((m, d), F32),
        scratch_shapes=[pltpu.VMEM((FFN_TM, d), BF16)],
        compiler_params=_params(("parallel", "arbitrary")),
        name="macaron_ffn",
    )(x, g.reshape(1, d), wg, wu, wd)


PROJ_TM = 1024
PROJ_TN = 512
QKV_SUB_BLOCKS = 4


def _qkv_kernel(x_ref, g_ref, w_ref, cg_ref, o_ref, h_ref, *, n_norm_steps, sub_blocks):
    j = pl.program_id(1)

    @pl.when(j == 0)
    def _():
        h_ref[...] = (_rms_normalise(x_ref[...]) * g_ref[...]).astype(BF16)

    def project(blk):
        cols = slice(blk * PROJ_TN, (blk + 1) * PROJ_TN)
        return _dot(h_ref[...], w_ref[:, cols])

    @pl.when(j < n_norm_steps)
    def _():
        for blk in range(sub_blocks):
            y = project(blk)
            for c in range(PROJ_TN // HEAD_DIM):
                cols = slice(blk * PROJ_TN + c * HEAD_DIM, blk * PROJ_TN + (c + 1) * HEAD_DIM)
                head = slice(c * HEAD_DIM, (c + 1) * HEAD_DIM)
                o_ref[:, cols] = (_rms_normalise(y[:, head]) * cg_ref[:, cols]).astype(BF16)

    @pl.when(j >= n_norm_steps)
    def _():
        for blk in range(sub_blocks):
            o_ref[:, blk * PROJ_TN:(blk + 1) * PROJ_TN] = project(blk).astype(BF16)


def _qkv(x, g, w, layer, col_gain, n_norm_cols, n=None, sub_blocks=1):
    m, d = x.shape
    n = w.shape[2] if n is None else n
    tn = sub_blocks * PROJ_TN
    assert n % tn == 0 and n_norm_cols % tn == 0
    kern = functools.partial(_qkv_kernel, n_norm_steps=n_norm_cols // tn, sub_blocks=sub_blocks)
    return pl.pallas_call(
        kern,
        grid=(m // PROJ_TM, n // tn),
        in_specs=[
            pl.BlockSpec((PROJ_TM, d), lambda i, j: (i, 0)),
            pl.BlockSpec((1, d), lambda i, j: (0, 0)),
            pl.BlockSpec((None, d, tn), lambda i, j: (layer, 0, j)),
            pl.BlockSpec((1, tn), lambda i, j: (0, j)),
        ],
        out_specs=pl.BlockSpec((PROJ_TM, tn), lambda i, j: (i, j)),
        out_shape=jax.ShapeDtypeStruct((m, n), BF16),
        scratch_shapes=[pltpu.VMEM((PROJ_TM, d), BF16)],
        compiler_params=_params(("parallel", "arbitrary")),
        name="mixer_qkv",
    )(x, g.reshape(1, d), w, col_gain)


A_TK = 512


def _proj_t_kernel(x_ref, g_ref, w_ref, o_ref, h_ref):
    @pl.when(pl.program_id(1) == 0)
    def _():
        h_ref[...] = (_rms_normalise(x_ref[...]) * g_ref[...]).astype(BF16)

    yt = _dot(h_ref[...], w_ref[...]).T.astype(BF16)
    for r in range(PROJ_TM // A_TK):
        o_ref[r] = yt[:, r * A_TK:(r + 1) * A_TK]


def _proj_t(x, g, w, layer, col0):
    m, d = x.shape
    n = w.shape[2] - col0
    j0 = col0 // PROJ_TN
    return pl.pallas_call(
        _proj_t_kernel,
        grid=(m // PROJ_TM, n // PROJ_TN),
        in_specs=[
            pl.BlockSpec((PROJ_TM, d), lambda i, j: (i, 0)),
            pl.BlockSpec((1, d), lambda i, j: (0, 0)),
            pl.BlockSpec((None, d, PROJ_TN), lambda i, j: (layer, 0, j0 + j)),
        ],
        out_specs=pl.BlockSpec((PROJ_TM // A_TK, PROJ_TN, A_TK), lambda i, j: (i, j, 0)),
        out_shape=jax.ShapeDtypeStruct((m // A_TK, n, A_TK), BF16),
        scratch_shapes=[pltpu.VMEM((PROJ_TM, d), BF16)],
        compiler_params=_params(("parallel", "arbitrary")),
        name="mixer_v_transposed",
    )(x, g.reshape(1, d), w)


def _out_proj_kernel(a_ref, w_ref, x_ref, o_ref):
    o_ref[...] = x_ref[...] + _dot(a_ref[...], w_ref[...])


OUT_TM = 512


def _out_proj(a, w, layer, x):
    m, k = a.shape
    n = w.shape[2]
    return pl.pallas_call(
        _out_proj_kernel,
        grid=(m // OUT_TM,),
        in_specs=[
            pl.BlockSpec((OUT_TM, k), lambda i: (i, 0)),
            pl.BlockSpec((None, k, n), lambda i: (layer, 0, 0)),
            pl.BlockSpec((OUT_TM, n), lambda i: (i, 0)),
        ],
        out_specs=pl.BlockSpec((OUT_TM, n), lambda i: (i, 0)),
        out_shape=jax.ShapeDtypeStruct((m, n), F32),
        compiler_params=_params(("parallel",)),
        name="mixer_out_proj",
    )(a, w, x)


A_TQ = 512
A_MAX_FIXED_SHIFT = 40.0


def _diff_attn_kernel(scal_ref, slope_ref, q_ref, k_ref, vt_ref, sg_ref, *refs, out_scale, n_cast):
    cast_src, o_ref, cast_dst = refs[:n_cast], refs[n_cast], refs[n_cast + 1:2 * n_cast + 1]
    (qt_ref, tbl_ref, s0_ref, s1_ref, x0_ref, x1_ref, p0_ref, p1_ref, a0_ref, a1_ref,
     m_ref, l_ref, acc_ref) = refs[2 * n_cast + 1:]
    for src_ref, dst_ref in zip(cast_src, cast_dst):
        dst_ref[...] = src_ref[...].astype(BF16)

    h = pl.program_id(1)
    qi = pl.program_id(2)
    n_kv = vt_ref.shape[0]
    kv_per_q = A_TQ // A_TK
    c = slope_ref[h] * LOG2E

    @pl.when(qi == 0)
    def _():
        rows = lax.broadcasted_iota(jnp.int32, (A_TK, A_TQ), 0)
        cols = lax.broadcasted_iota(jnp.int32, (A_TK, A_TQ), 1)
        cd = c * (rows - cols).astype(F32)
        tbl_ref[0] = cd
        tbl_ref[1] = -cd
        for d in range(kv_per_q):
            tbl_ref[2 + d] = -jnp.abs(cd + c * float(d * A_TK))

    qt_ref[...] = q_ref[...].astype(F32).T.astype(BF16)
    m_ref[...] = jnp.full(m_ref.shape, NEG_BIG, F32)
    l_ref[...] = jnp.zeros(l_ref.shape, F32)
    acc_ref[...] = jnp.zeros(acc_ref.shape, F32)
    p1_ref[...] = jnp.zeros(p1_ref.shape, BF16)
    a1_ref[...] = jnp.ones(a1_ref.shape, F32)

    def bias_of(j):
        d = j - qi * kv_per_q
        off = c * (d * A_TK).astype(F32)
        idx = jnp.where(d < 0, 0, jnp.where(d >= kv_per_q, 1, 2 + d))
        kappa = jnp.where(d < 0, off, jnp.where(d >= kv_per_q, -off, 0.0))
        return idx, kappa

    def biased_logits(j, half, idx):
        k0 = pl.multiple_of(j * A_TK, A_TK)
        cols_h = slice(half * HEAD_DIM, (half + 1) * HEAD_DIM)
        return _dot(k_ref[pl.ds(k0, A_TK), cols_h], qt_ref[cols_h, :]) + tbl_ref[idx]

    bound = scal_ref[1]
    fixed_shift_ok = bound <= A_MAX_FIXED_SHIFT

    @pl.when(fixed_shift_ok)
    def _():
        def probabilities(j, p_ref, live):
            idx, kappa = bias_of(j)
            shift = kappa - bound
            for half in range(2):
                p = jnp.exp2(biased_logits(j, half, idx) + shift)
                l_ref[half] += live * jnp.sum(p, axis=0, keepdims=True)
                p_ref[half] = p.astype(BF16)

        def values(j, p_ref):
            vtb = vt_ref[j]
            for half in range(2):
                acc_ref[half] += _dot(vtb, p_ref[half])

        probabilities(0, p0_ref, 1.0)

        def pair(jj, carry):
            j = 2 * jj
            probabilities(j + 1, p1_ref, 1.0)
            values(j, p0_ref)
            nxt = j + 2
            probabilities(jnp.minimum(nxt, n_kv - 1), p0_ref, (nxt < n_kv).astype(F32))
            values(j + 1, p1_ref)
            return carry

        lax.fori_loop(0, n_kv // 2, pair, 0)

    @pl.when(jnp.logical_not(fixed_shift_ok))
    def _():
        def logits(j, z_ref, zmax_ref):
            idx, _ = bias_of(j)
            for half in range(2):
                z = biased_logits(j, half, idx)
                z_ref[half] = z
                zmax_ref[half] = jnp.max(z, axis=0, keepdims=True)

        def softmax(j, z_ref, zmax_ref, p_ref, a_ref):
            _, kappa = bias_of(j)
            for half in range(2):
                m_old = m_ref[half]
                m_new = jnp.maximum(m_old, zmax_ref[half] + kappa)
                p = jnp.exp2(z_ref[half] - (m_new - kappa))
                alpha = jnp.exp2(m_old - m_new)
                l_ref[half] = alpha * l_ref[half] + jnp.sum(p, axis=0, keepdims=True)
                m_ref[half] = m_new
                a_ref[half] = alpha
                p_ref[half] = p.astype(BF16)

        def values(j, p_ref, a_ref):
            vtb = vt_ref[j]
            for half in range(2):
                acc_ref[half] = a_ref[half] * acc_ref[half] + _dot(vtb, p_ref[half])

        logits(0, s0_ref, x0_ref)

        def pair(jj, carry):
            j = 2 * jj
            logits(j + 1, s1_ref, x1_ref)
            softmax(j, s0_ref, x0_ref, p0_ref, a0_ref)
            values(jnp.maximum(j - 1, 0), p1_ref, a1_ref)
            logits(jnp.minimum(j + 2, n_kv - 1), s0_ref, x0_ref)
            softmax(j + 1, s1_ref, x1_ref, p1_ref, a1_ref)
            values(j, p0_ref, a0_ref)
            return carry

        lax.fori_loop(0, n_kv // 2, pair, 0)
        values(n_kv - 1, p1_ref, a1_ref)

    ot = acc_ref[0] * (1.0 / l_ref[0]) - scal_ref[0] * (acc_ref[1] * (1.0 / l_ref[1]))
    inv_rms = lax.rsqrt(jnp.mean(ot * ot, axis=0, keepdims=True) + RMS_EPS)
    ot = ot * inv_rms * (sg_ref[...] * out_scale)
    o_ref[...] = ot.T.astype(BF16)


def _logit_bound(q_gain, k_gain):
    q_scale = HEAD_DIM ** -0.5 * LOG2E
    return (HEAD_DIM * q_scale * 1.02) * jnp.max(jnp.abs(q_gain.astype(F32))) * jnp.max(jnp.abs(k_gain.astype(F32)))


def _cast_tiling(shape, n_steps):
    n_rows, width = math.prod(shape[:-1]), shape[-1]
    steps_per_tile = 1
    while n_rows * steps_per_tile % (n_steps * 2 * SUBLANE) != 0:
        steps_per_tile *= 2
    assert n_steps % steps_per_tile == 0 and width % LANE == 0
    return (n_rows * steps_per_tile // n_steps, width), steps_per_tile


def _diff_attention(qk, vt, lam, bound, subln_g, batch, seq, lambda_init, cast=()):
    assert A_TQ % A_TK == 0 and seq % A_TQ == 0
    assert (seq // A_TK) % 2 == 0
    m = qk.shape[0]
    n_q = seq // A_TQ
    n_kv = seq // A_TK
    n_steps = batch * A_HEADS * n_q
    slopes = jnp.asarray(2.0 ** (-8.0 * np.arange(1, A_HEADS + 1) / A_HEADS), F32)
    kern = functools.partial(_diff_attn_kernel, out_scale=1.0 - lambda_init, n_cast=len(cast))
    smem = pl.BlockSpec(memory_space=pltpu.SMEM)
    def cast_spec(tile, steps_per_tile):
        return pl.BlockSpec(tile, lambda b, h, i: (((b * A_HEADS + h) * n_q + i) // steps_per_tile, 0))

    cast_specs = [cast_spec(*_cast_tiling(w.shape, n_steps)) for w in cast]
    cast_views = [w.reshape(-1, w.shape[-1]) for w in cast]
    out = pl.pallas_call(
        kern,
        grid=(batch, A_HEADS, n_q),
        in_specs=[
            smem,
            smem,
            pl.BlockSpec((A_TQ, A_VDIM), lambda b, h, i: (b * n_q + i, h)),
            pl.BlockSpec((seq, A_VDIM), lambda b, h, i: (b, A_HEADS + h)),
            pl.BlockSpec((n_kv, A_VDIM, A_TK), lambda b, h, i: (b, h, 0)),
            pl.BlockSpec((A_VDIM, 1), lambda b, h, i: (0, 0)),
        ] + cast_specs,
        out_specs=[pl.BlockSpec((A_TQ, A_VDIM), lambda b, h, i: (b * n_q + i, h))] + cast_specs,
        out_shape=[jax.ShapeDtypeStruct((m, A_HEADS * A_VDIM), BF16)]
        + [jax.ShapeDtypeStruct(v.shape, BF16) for v in cast_views],
        scratch_shapes=[
            pltpu.VMEM((A_VDIM, A_TQ), BF16),
            pltpu.VMEM((2 + A_TQ // A_TK, A_TK, A_TQ), F32),
            pltpu.VMEM((2, A_TK, A_TQ), F32),
            pltpu.VMEM((2, A_TK, A_TQ), F32),
            pltpu.VMEM((2, 1, A_TQ), F32),
            pltpu.VMEM((2, 1, A_TQ), F32),
            pltpu.VMEM((2, A_TK, A_TQ), BF16),
            pltpu.VMEM((2, A_TK, A_TQ), BF16),
            pltpu.VMEM((2, 1, A_TQ), F32),
            pltpu.VMEM((2, 1, A_TQ), F32),
            pltpu.VMEM((2, 1, A_TQ), F32),
            pltpu.VMEM((2, 1, A_TQ), F32),
            pltpu.VMEM((2, A_VDIM, A_TQ), F32),
        ],
        compiler_params=_params(("parallel", "parallel", "arbitrary")),
        name="diff_attention",
    )(jnp.stack([lam, bound]).astype(F32), slopes, qk, qk, vt, subln_g.reshape(A_VDIM, 1), *cast_views)
    return out[0], tuple(c.reshape(w.shape) for c, w in zip(out[1:], cast))


B_QROWS = 8
B_KROWS = 16
B_TQ = B_QROWS * GRID_W
B_TK = B_KROWS * GRID_W


def _na_bias_table(rel_bias, rows):
    kr = min(NA_ROWS, rows)
    n_groups = rows // B_QROWS
    n_heads, n_drow, n_dcol = rel_bias.shape
    c = np.arange(GRID_W)
    c_start = np.clip(c - NA_COLS // 2, 0, GRID_W - NA_COLS)
    kc = np.arange(GRID_W)
    col_ok = (kc[None, :] >= c_start[:, None]) & (kc[None, :] < c_start[:, None] + NA_COLS)
    dcol = kc[None, :] - c[:, None] + (NA_COLS - 1)
    select = (np.arange(n_dcol)[:, None, None] == dcol[None]) & col_ok[None]
    cols = jnp.einsum("hrd,dck->hcrk", rel_bias * LOG2E, jnp.asarray(select, F32),
                      precision=lax.Precision.HIGHEST)
    strip = jnp.where(col_ok[None, :, None, :], cols, NEG_BIG).reshape(n_heads, GRID_W, n_drow * GRID_W)
    row_blocks = []
    for g in (0, 1, n_groups - 1):
        k_row0 = int(np.clip(g * B_QROWS - kr // 2, 0, rows - B_KROWS))
        for r in range(g * B_QROWS, (g + 1) * B_QROWS):
            r_start = int(np.clip(r - kr // 2, 0, rows - kr))
            lead = r_start - k_row0
            d0 = r_start - r + (NA_ROWS - 1)
            parts = [jnp.full((n_heads, GRID_W, lead * GRID_W), NEG_BIG, F32),
                     strip[:, :, d0 * GRID_W:(d0 + kr) * GRID_W],
                     jnp.full((n_heads, GRID_W, (B_KROWS - kr - lead) * GRID_W), NEG_BIG, F32)]
            row_blocks.append(jnp.concatenate(parts, axis=-1))
    return jnp.stack(row_blocks, axis=1).reshape(n_heads, 3, B_TQ, B_TK)


def _na_kernel(q_ref, k_ref, v_ref, bias_ref, o_ref, p0_ref, p1_ref, l0_ref, l1_ref, *, rows):
    n_groups = rows // B_QROWS

    def key_start(g):
        k_row0 = jnp.clip(g * B_QROWS - NA_ROWS // 2, 0, rows - B_KROWS)
        return pl.multiple_of(k_row0 * GRID_W, (NA_ROWS // 2) * GRID_W)

    def probabilities(g, p_ref, l_ref):
        q0 = pl.multiple_of(g * B_TQ, B_TQ)
        variant = jnp.where(g == 0, 0, jnp.where(g == n_groups - 1, 2, 1))
        s = _dot_nt(q_ref[pl.ds(q0, B_TQ), :], k_ref[pl.ds(key_start(g), B_TK), :]) + bias_ref[0, variant]
        p = jnp.exp2(s - jnp.max(s, axis=-1, keepdims=True))
        l_ref[...] = jnp.sum(p, axis=-1, keepdims=True)
        p_ref[...] = p.astype(BF16)

    def values(g, p_ref, l_ref):
        q0 = pl.multiple_of(g * B_TQ, B_TQ)
        o = _dot(p_ref[...], v_ref[pl.ds(key_start(g), B_TK), :])
        o_ref[pl.ds(q0, B_TQ), :] = (o * (1.0 / l_ref[...])).astype(BF16)

    probabilities(0, p0_ref, l0_ref)

    def pair(gg, carry):
        g = 2 * gg
        probabilities(g + 1, p1_ref, l1_ref)
        values(g, p0_ref, l0_ref)
        probabilities(jnp.minimum(g + 2, n_groups - 1), p0_ref, l0_ref)
        values(g + 1, p1_ref, l1_ref)
        return carry

    lax.fori_loop(0, n_groups // 2, pair, 0)


def _neighbourhood_attention(qkv, bias, batch, seq):
    m = qkv.shape[0]
    rows = seq // GRID_W
    assert rows % (2 * B_QROWS) == 0 and rows >= B_KROWS and NA_ROWS <= rows
    kern = functools.partial(_na_kernel, rows=rows)
    return pl.pallas_call(
        kern,
        grid=(batch, B_HEADS),
        in_specs=[
            pl.BlockSpec((seq, HEAD_DIM), lambda b, h: (b, h)),
            pl.BlockSpec((seq, HEAD_DIM), lambda b, h: (b, B_HEADS + h)),
            pl.BlockSpec((seq, HEAD_DIM), lambda b, h: (b, 2 * B_HEADS + h)),
            pl.BlockSpec((1, 3, B_TQ, B_TK), lambda b, h: (h, 0, 0, 0)),
        ],
        out_specs=pl.BlockSpec((seq, HEAD_DIM), lambda b, h: (b, h)),
        out_shape=jax.ShapeDtypeStruct((m, B_HEADS * HEAD_DIM), BF16),
        scratch_shapes=[
            pltpu.VMEM((B_TQ, B_TK), BF16),
            pltpu.VMEM((B_TQ, B_TK), BF16),
            pltpu.VMEM((B_TQ, 1), F32),
            pltpu.VMEM((B_TQ, 1), F32),
        ],
        compiler_params=_params(("parallel", "arbitrary")),
        name="neighbourhood_attention",
    )(qkv, qkv, qkv, bias)


C_TQ = 256
C_TK = C_TQ + 2 * C_WINDOW


def _swa_kernel(slope_ref, sink_ref, q_ref, k_ref, v_ref, o_ref, p0_ref, p1_ref, l0_ref, l1_ref):
    kv = pl.program_id(1)
    seq = k_ref.shape[0]
    n_tiles = seq // C_TQ
    rows = lax.broadcasted_iota(jnp.int32, (C_TQ, C_TK), 0)
    cols = lax.broadcasted_iota(jnp.int32, (C_TQ, C_TK), 1)
    delta = rows - cols

    def key_start(t0):
        return pl.multiple_of(jnp.clip(t0 - C_WINDOW, 0, seq - C_TK), C_WINDOW)

    def probabilities(i, p_ref, l_ref):
        t0 = pl.multiple_of(i * C_TQ, C_TQ)
        k0 = key_start(t0)
        dist = jnp.abs(delta + (t0 - k0))
        valid = dist <= C_WINDOW
        dist_f = dist.astype(F32)
        kb = k_ref[pl.ds(k0, C_TK), :]
        for g in range(C_GROUP):
            head = kv * C_GROUP + g
            cols_g = slice(g * HEAD_DIM, (g + 1) * HEAD_DIM)
            s = _dot_nt(q_ref[pl.ds(t0, C_TQ), cols_g], kb)
            s = jnp.where(valid, s - (slope_ref[head] * LOG2E) * dist_f, NEG_BIG)
            sink = sink_ref[head] * LOG2E
            m = jnp.maximum(jnp.max(s, axis=-1, keepdims=True), sink)
            p = jnp.exp2(s - m)
            l_ref[g] = jnp.sum(p, axis=-1, keepdims=True) + jnp.exp2(sink - m)
            p_ref[g] = p.astype(BF16)

    def values(i, p_ref, l_ref):
        t0 = pl.multiple_of(i * C_TQ, C_TQ)
        vb = v_ref[pl.ds(key_start(t0), C_TK), :]
        for g in range(C_GROUP):
            cols_g = slice(g * HEAD_DIM, (g + 1) * HEAD_DIM)
            o_ref[pl.ds(t0, C_TQ), cols_g] = (_dot(p_ref[g], vb) * (1.0 / l_ref[g])).astype(BF16)

    probabilities(0, p0_ref, l0_ref)

    def pair(ii, carry):
        i = 2 * ii
        probabilities(i + 1, p1_ref, l1_ref)
        values(i, p0_ref, l0_ref)
        probabilities(jnp.minimum(i + 2, n_tiles - 1), p0_ref, l0_ref)
        values(i + 1, p1_ref, l1_ref)
        return carry

    lax.fori_loop(0, n_tiles // 2, pair, 0)


def _window_attention(qkv, sink, batch, seq):
    m = qkv.shape[0]
    assert seq % C_TQ == 0 and seq >= C_TK
    slopes = jnp.asarray(2.0 ** (-8.0 * np.arange(1, C_HEADS + 1) / C_HEADS), F32)
    gw = C_GROUP * HEAD_DIM
    smem = pl.BlockSpec(memory_space=pltpu.SMEM)
    return pl.pallas_call(
        _swa_kernel,
        grid=(batch, C_KV_HEADS),
        in_specs=[
            smem,
            smem,
            pl.BlockSpec((seq, gw), lambda b, kv: (b, kv)),
            pl.BlockSpec((seq, HEAD_DIM), lambda b, kv: (b, C_HEADS + kv)),
            pl.BlockSpec((seq, HEAD_DIM), lambda b, kv: (b, C_HEADS + C_KV_HEADS + kv)),
        ],
        out_specs=pl.BlockSpec((seq, gw), lambda b, kv: (b, kv)),
        out_shape=jax.ShapeDtypeStruct((m, C_HEADS * HEAD_DIM), BF16),
        scratch_shapes=[
            pltpu.VMEM((C_GROUP, C_TQ, C_TK), BF16),
            pltpu.VMEM((C_GROUP, C_TQ, C_TK), BF16),
            pltpu.VMEM((C_GROUP, C_TQ, 1), F32),
            pltpu.VMEM((C_GROUP, C_TQ, 1), F32),
        ],
        compiler_params=_params(("parallel", "arbitrary")),
        name="window_attention",
    )(slopes, sink.astype(F32), qkv, qkv, qkv)


def _col_gain(q_gain, k_gain, n_q_heads, n_k_heads, n_cols):
    q_scale = HEAD_DIM ** -0.5 * LOG2E
    parts = [jnp.tile(q_gain.astype(F32) * q_scale, n_q_heads), jnp.tile(k_gain.astype(F32), n_k_heads)]
    gain = jnp.concatenate(parts)
    return jnp.pad(gain, (0, n_cols - gain.shape[0]), constant_values=1.0).reshape(1, n_cols)


def kernel(x, norm_ffn1, ffn1_w_gate, ffn1_w_up, ffn1_w_down, norm_mix, norm_ffn2, ffn2_w_gate, ffn2_w_up, ffn2_w_down, a_w_qkv, a_q_norm, a_k_norm, a_lambda_q1, a_lambda_k1, a_lambda_q2, a_lambda_k2, a_subln, a_w_o, b_w_qkv, b_q_norm, b_k_norm, b_rel_bias, b_w_o, c_w_qkv, c_q_norm, c_k_norm, c_sink, c_w_o):
    batch, seq, d = x.shape
    x = x.reshape(batch * seq, d)
    first_slab = lambda w: w[:1].astype(BF16)
    ffn1 = tuple(map(first_slab, (ffn1_w_gate, ffn1_w_up, ffn1_w_down)))
    ffn2 = None
    later_f32 = (ffn1_w_gate, ffn1_w_up, ffn1_w_down, ffn2_w_gate, ffn2_w_up, ffn2_w_down,
                 a_w_qkv, a_w_o, b_w_qkv, b_w_o, c_w_qkv, c_w_o)
    a_w_qkv, a_w_o = first_slab(a_w_qkv), first_slab(a_w_o)

    for i in range(DEPTH):
        x = _ffn(x, norm_ffn1[i], *ffn1, i)
        kind, j = i % N_MIXERS, i // N_MIXERS
        if kind == 0:
            lambda_init = 0.8 - 0.6 * math.exp(-0.3 * i)
            lam = (jnp.exp(jnp.sum(a_lambda_q1[j].astype(F32) * a_lambda_k1[j].astype(F32)))
                   - jnp.exp(jnp.sum(a_lambda_q2[j].astype(F32) * a_lambda_k2[j].astype(F32))) + lambda_init)
            n_qk = 2 * A_HEADS * 2 * HEAD_DIM
            gain = _col_gain(a_q_norm[j], a_k_norm[j], 2 * A_HEADS, 2 * A_HEADS, n_qk)
            qk = _qkv(x, norm_mix[i], a_w_qkv, j, gain, n_qk, n=n_qk, sub_blocks=QKV_SUB_BLOCKS)
            vt = _proj_t(x, norm_mix[i], a_w_qkv, j, n_qk)
            bound = _logit_bound(a_q_norm[j], a_k_norm[j])
            o, cast = _diff_attention(qk, vt, lam, bound, a_subln[j], batch, seq, lambda_init,
                                      cast=later_f32 if i == 0 else ())
            x = _out_proj(o, a_w_o, j, x)
            if i == 0:
                ffn1, ffn2 = cast[:3], cast[3:6]
                a_w_qkv, a_w_o, b_w_qkv, b_w_o, c_w_qkv, c_w_o = cast[6:]
        elif kind == 1:
            n_qk = 2 * B_HEADS * HEAD_DIM
            gain = _col_gain(b_q_norm[j], b_k_norm[j], B_HEADS, B_HEADS, b_w_qkv.shape[2])
            qkv = _qkv(x, norm_mix[i], b_w_qkv, j, gain, n_qk, sub_blocks=QKV_SUB_BLOCKS)
            bias = _na_bias_table(b_rel_bias[j].astype(F32), seq // GRID_W)
            o = _neighbourhood_attention(qkv, bias, batch, seq)
            x = _out_proj(o, b_w_o, j, x)
        else:
            n_qk = (C_HEADS + C_KV_HEADS) * HEAD_DIM
            gain = _col_gain(c_q_norm[j], c_k_norm[j], C_HEADS, C_KV_HEADS, c_w_qkv.shape[2])
            qkv = _qkv(x, norm_mix[i], c_w_qkv, j, gain, n_qk)
            o = _window_attention(qkv, c_sink[j], batch, seq)
            x = _out_proj(o, c_w_o, j, x)
        x = _ffn(x, norm_ffn2[i], *ffn2, i)
    return x.reshape(batch, seq, d)
```

```python
import functools
import math

import jax
import jax.numpy as jnp
import numpy as np
from jax import lax
from jax.experimental import pallas as pl
from jax.experimental.pallas import tpu as pltpu

D_MODEL = 2048
DEPTH = 4
N_MIXERS = 3
HEAD_DIM = 128
RMS_EPS = 1e-6
A_HEADS = D_MODEL // (2 * HEAD_DIM)
A_VDIM = 2 * HEAD_DIM
B_HEADS = D_MODEL // HEAD_DIM
GRID_W = 64
NA_ROWS = 8
NA_COLS = 16
C_HEADS = D_MODEL // HEAD_DIM
C_KV_HEADS = 4
C_GROUP = C_HEADS // C_KV_HEADS
C_WINDOW = 128

LOG2E = math.log2(math.e)
NEG_BIG = -1e30
BF16 = jnp.bfloat16
F32 = jnp.float32

V7X_VMEM_BYTES = 64 * 1024 * 1024
VMEM_LIMIT = V7X_VMEM_BYTES - 8 * 1024 * 1024
LANE = 128
SUBLANE = 8


def _params(semantics):
    return pltpu.CompilerParams(dimension_semantics=semantics, vmem_limit_bytes=VMEM_LIMIT)


def _dot(a, b):
    return jnp.dot(a, b, preferred_element_type=F32)


def _dot_nt(a, b):
    return lax.dot_general(a, b, (((1,), (1,)), ((), ())), preferred_element_type=F32)


def _rms_normalise(x):
    return x * lax.rsqrt(jnp.mean(x * x, axis=-1, keepdims=True) + RMS_EPS)


FFN_TM = 1024
FFN_TF = 512


def _ffn_kernel(x_ref, g_ref, wg_ref, wu_ref, wd_ref, o_ref, h_ref):
    j = pl.program_id(1)

    @pl.when(j == 0)
    def _():
        x = x_ref[...]
        h_ref[...] = (_rms_normalise(x) * g_ref[...]).astype(BF16)
        o_ref[...] = x

    h = h_ref[...]
    gate = _dot(h, wg_ref[...])
    up = _dot(h, wu_ref[...])
    act = gate * (0.5 / (1.0 + jnp.exp(-gate))) * up
    o_ref[...] += _dot(act.astype(BF16), wd_ref[...])


def _ffn(x, g, wg, wu, wd, layer):
    m, d = x.shape
    f = wg.shape[2]
    return pl.pallas_call(
        _ffn_kernel,
        grid=(m // FFN_TM, f // FFN_TF),
        in_specs=[
            pl.BlockSpec((FFN_TM, d), lambda i, j: (i, 0)),
            pl.BlockSpec((1, d), lambda i, j: (0, 0)),
            pl.BlockSpec((None, d, FFN_TF), lambda i, j: (layer, 0, j)),
            pl.BlockSpec((None, d, FFN_TF), lambda i, j: (layer, 0, j)),
            pl.BlockSpec((None, FFN_TF, d), lambda i, j: (layer, j, 0)),
        ],
        out_specs=pl.BlockSpec((FFN_TM, d), lambda i, j: (i, 0)),
        out_shape=jax.ShapeDtypeStruct((m, d), F32),
        scratch_shapes=[pltpu.VMEM((FFN_TM, d), BF16)],
        compiler_params=_params(("parallel", "arbitrary")),
        name="macaron_ffn",
    )(x, g.reshape(1, d), wg, wu, wd)


PROJ_TM = 1024
PROJ_TN = 512
QKV_SUB_BLOCKS = 4


def _qkv_kernel(x_ref, g_ref, w_ref, cg_ref, o_ref, h_ref, *, n_norm_steps, sub_blocks):
    j = pl.program_id(1)

    @pl.when(j == 0)
    def _():
        h_ref[...] = (_rms_normalise(x_ref[...]) * g_ref[...]).astype(BF16)

    def project(blk):
        cols = slice(blk * PROJ_TN, (blk + 1) * PROJ_TN)
        return _dot(h_ref[...], w_ref[:, cols])

    @pl.when(j < n_norm_steps)
    def _():
        for blk in range(sub_blocks):
            y = project(blk)
            for c in range(PROJ_TN // HEAD_DIM):
                cols = slice(blk * PROJ_TN + c * HEAD_DIM, blk * PROJ_TN + (c + 1) * HEAD_DIM)
                head = slice(c * HEAD_DIM, (c + 1) * HEAD_DIM)
                o_ref[:, cols] = (_rms_normalise(y[:, head]) * cg_ref[:, cols]).astype(BF16)

    @pl.when(j >= n_norm_steps)
    def _():
        for blk in range(sub_blocks):
            o_ref[:, blk * PROJ_TN:(blk + 1) * PROJ_TN] = project(blk).astype(BF16)


def _qkv(x, g, w, layer, col_gain, n_norm_cols, n=None, sub_blocks=1):
    m, d = x.shape
    n = w.shape[2] if n is None else n
    tn = sub_blocks * PROJ_TN
    assert n % tn == 0 and n_norm_cols % tn == 0
    kern = functools.partial(_qkv_kernel, n_norm_steps=n_norm_cols // tn, sub_blocks=sub_blocks)
    return pl.pallas_call(
        kern,
        grid=(m // PROJ_TM, n // tn),
        in_specs=[
            pl.BlockSpec((PROJ_TM, d), lambda i, j: (i, 0)),
            pl.BlockSpec((1, d), lambda i, j: (0, 0)),
            pl.BlockSpec((None, d, tn), lambda i, j: (layer, 0, j)),
            pl.BlockSpec((1, tn), lambda i, j: (0, j)),
        ],
        out_specs=pl.BlockSpec((PROJ_TM, tn), lambda i, j: (i, j)),
        out_shape=jax.ShapeDtypeStruct((m, n), BF16),
        scratch_shapes=[pltpu.VMEM((PROJ_TM, d), BF16)],
        compiler_params=_params(("parallel", "arbitrary")),
        name="mixer_qkv",
    )(x, g.reshape(1, d), w, col_gain)


A_TK = 512


def _proj_t_kernel(x_ref, g_ref, w_ref, o_ref, h_ref):
    @pl.when(pl.program_id(1) == 0)
    def _():
        h_ref[...] = (_rms_normalise(x_ref[...]) * g_ref[...]).astype(BF16)

    yt = _dot(h_ref[...], w_ref[...]).T.astype(BF16)
    for r in range(PROJ_TM // A_TK):
        o_ref[r] = yt[:, r * A_TK:(r + 1) * A_TK]


def _proj_t(x, g, w, layer, col0):
    m, d = x.shape
    n = w.shape[2] - col0
    j0 = col0 // PROJ_TN
    return pl.pallas_call(
        _proj_t_kernel,
        grid=(m // PROJ_TM, n // PROJ_TN),
        in_specs=[
            pl.BlockSpec((PROJ_TM, d), lambda i, j: (i, 0)),
            pl.BlockSpec((1, d), lambda i, j: (0, 0)),
            pl.BlockSpec((None, d, PROJ_TN), lambda i, j: (layer, 0, j0 + j)),
        ],
        out_specs=pl.BlockSpec((PROJ_TM // A_TK, PROJ_TN, A_TK), lambda i, j: (i, j, 0)),
        out_shape=jax.ShapeDtypeStruct((m // A_TK, n, A_TK), BF16),
        scratch_shapes=[pltpu.VMEM((PROJ_TM, d), BF16)],
        compiler_params=_params(("parallel", "arbitrary")),
        name="mixer_v_transposed",
    )(x, g.reshape(1, d), w)


def _out_proj_kernel(a_ref, w_ref, x_ref, o_ref):
    o_ref[...] = x_ref[...] + _dot(a_ref[...], w_ref[...])


OUT_TM = 512


def _out_proj(a, w, layer, x):
    m, k = a.shape
    n = w.shape[2]
    return pl.pallas_call(
        _out_proj_kernel,
        grid=(m // OUT_TM,),
        in_specs=[
            pl.BlockSpec((OUT_TM, k), lambda i: (i, 0)),
            pl.BlockSpec((None, k, n), lambda i: (layer, 0, 0)),
            pl.BlockSpec((OUT_TM, n), lambda i: (i, 0)),
        ],
        out_specs=pl.BlockSpec((OUT_TM, n), lambda i: (i, 0)),
        out_shape=jax.ShapeDtypeStruct((m, n), F32),
        compiler_params=_params(("parallel",)),
        name="mixer_out_proj",
    )(a, w, x)


A_TQ = 512
A_MAX_FIXED_SHIFT = 40.0


def _diff_attn_kernel(scal_ref, slope_ref, q_ref, k_ref, vt_ref, sg_ref, *refs, out_scale, n_cast):
    cast_src, o_ref, cast_dst = refs[:n_cast], refs[n_cast], refs[n_cast + 1:2 * n_cast + 1]
    (qt_ref, tbl_ref, s0_ref, s1_ref, x0_ref, x1_ref, p0_ref, p1_ref, a0_ref, a1_ref,
     m_ref, l_ref, acc_ref) = refs[2 * n_cast + 1:]
    for src_ref, dst_ref in zip(cast_src, cast_dst):
        dst_ref[...] = src_ref[...].astype(BF16)

    h = pl.program_id(1)
    qi = pl.program_id(2)
    n_kv = vt_ref.shape[0]
    kv_per_q = A_TQ // A_TK
    c = slope_ref[h] * LOG2E

    @pl.when(qi == 0)
    def _():
        rows = lax.broadcasted_iota(jnp.int32, (A_TK, A_TQ), 0)
        cols = lax.broadcasted_iota(jnp.int32, (A_TK, A_TQ), 1)
        cd = c * (rows - cols).astype(F32)
        tbl_ref[0] = cd
        tbl_ref[1] = -cd
        for d in range(kv_per_q):
            tbl_ref[2 + d] = -jnp.abs(cd + c * float(d * A_TK))

    qt_ref[...] = q_ref[...].astype(F32).T.astype(BF16)
    m_ref[...] = jnp.full(m_ref.shape, NEG_BIG, F32)
    l_ref[...] = jnp.zeros(l_ref.shape, F32)
    acc_ref[...] = jnp.zeros(acc_ref.shape, F32)
    p1_ref[...] = jnp.zeros(p1_ref.shape, BF16)
    a1_ref[...] = jnp.ones(a1_ref.shape, F32)

    def bias_of(j):
        d = j - qi * kv_per_q
        off = c * (d * A_TK).astype(F32)
        idx = jnp.where(d < 0, 0, jnp.where(d >= kv_per_q, 1, 2 + d))
        kappa = jnp.where(d < 0, off, jnp.where(d >= kv_per_q, -off, 0.0))
        return idx, kappa

    def biased_logits(j, half, idx):
        k0 = pl.multiple_of(j * A_TK, A_TK)
        cols_h = slice(half * HEAD_DIM, (half + 1) * HEAD_DIM)
        return _dot(k_ref[pl.ds(k0, A_TK), cols_h], qt_ref[cols_h, :]) + tbl_ref[idx]

    bound = scal_ref[1]
    fixed_shift_ok = bound <= A_MAX_FIXED_SHIFT

    @pl.when(fixed_shift_ok)
    def _():
        def probabilities(j, p_ref, live):
            idx, kappa = bias_of(j)
            shift = kappa - bound
            for half in range(2):
                p = jnp.exp2(biased_logits(j, half, idx) + shift)
                l_ref[half] += live * jnp.sum(p, axis=0, keepdims=True)
                p_ref[half] = p.astype(BF16)

        def values(j, p_ref):
            vtb = vt_ref[j]
            for half in range(2):
                acc_ref[half] += _dot(vtb, p_ref[half])

        probabilities(0, p0_ref, 1.0)

        def pair(jj, carry):
            j = 2 * jj
            probabilities(j + 1, p1_ref, 1.0)
            values(j, p0_ref)
            nxt = j + 2
            probabilities(jnp.minimum(nxt, n_kv - 1), p0_ref, (nxt < n_kv).astype(F32))
            values(j + 1, p1_ref)
            return carry

        lax.fori_loop(0, n_kv // 2, pair, 0)

    @pl.when(jnp.logical_not(fixed_shift_ok))
    def _():
        def logits(j, z_ref, zmax_ref):
            idx, _ = bias_of(j)
            for half in range(2):
                z = biased_logits(j, half, idx)
                z_ref[half] = z
                zmax_ref[half] = jnp.max(z, axis=0, keepdims=True)

        def softmax(j, z_ref, zmax_ref, p_ref, a_ref):
            _, kappa = bias_of(j)
            for half in range(2):
                m_old = m_ref[half]
                m_new = jnp.maximum(m_old, zmax_ref[half] + kappa)
                p = jnp.exp2(z_ref[half] - (m_new - kappa))
                alpha = jnp.exp2(m_old - m_new)
                l_ref[half] = alpha * l_ref[half] + jnp.sum(p, axis=0, keepdims=True)
                m_ref[half] = m_new
                a_ref[half] = alpha
                p_ref[half] = p.astype(BF16)

        def values(j, p_ref, a_ref):
            vtb = vt_ref[j]
            for half in range(2):
                acc_ref[half] = a_ref[half] * acc_ref[half] + _dot(vtb, p_ref[half])

        logits(0, s0_ref, x0_ref)

        def pair(jj, carry):
            j = 2 * jj
            logits(j + 1, s1_ref, x1_ref)
            softmax(j, s0_ref, x0_ref, p0_ref, a0_ref)
            values(jnp.maximum(j - 1, 0), p1_ref, a1_ref)
            logits(jnp.minimum(j + 2, n_kv - 1), s0_ref, x0_ref)
            softmax(j + 1, s1_ref, x1_ref, p1_ref, a1_ref)
            values(j, p0_ref, a0_ref)
            return carry

        lax.fori_loop(0, n_kv // 2, pair, 0)
        values(n_kv - 1, p1_ref, a1_ref)

    ot = acc_ref[0] * (1.0 / l_ref[0]) - scal_ref[0] * (acc_ref[1] * (1.0 / l_ref[1]))
    inv_rms = lax.rsqrt(jnp.mean(ot * ot, axis=0, keepdims=True) + RMS_EPS)
    ot = ot * inv_rms * (sg_ref[...] * out_scale)
    o_ref[...] = ot.T.astype(BF16)


def _logit_bound(q_gain, k_gain):
    q_scale = HEAD_DIM ** -0.5 * LOG2E
    return (HEAD_DIM * q_scale * 1.02) * jnp.max(jnp.abs(q_gain.astype(F32))) * jnp.max(jnp.abs(k_gain.astype(F32)))


def _cast_tiling(shape, n_steps):
    n_rows, width = math.prod(shape[:-1]), shape[-1]
    steps_per_tile = 1
    while n_rows * steps_per_tile % (n_steps * 2 * SUBLANE) != 0:
        steps_per_tile *= 2
    assert n_steps % steps_per_tile == 0 and width % LANE == 0
    return (n_rows * steps_per_tile // n_steps, width), steps_per_tile


def _diff_attention(qk, vt, lam, bound, subln_g, batch, seq, lambda_init, cast=()):
    assert A_TQ % A_TK == 0 and seq % A_TQ == 0
    assert (seq // A_TK) % 2 == 0
    m = qk.shape[0]
    n_q = seq // A_TQ
    n_kv = seq // A_TK
    n_steps = batch * A_HEADS * n_q
    slopes = jnp.asarray(2.0 ** (-8.0 * np.arange(1, A_HEADS + 1) / A_HEADS), F32)
    kern = functools.partial(_diff_attn_kernel, out_scale=1.0 - lambda_init, n_cast=len(cast))
    smem = pl.BlockSpec(memory_space=pltpu.SMEM)
    def cast_spec(tile, steps_per_tile):
        return pl.BlockSpec(tile, lambda b, h, i: (((b * A_HEADS + h) * n_q + i) // steps_per_tile, 0))

    cast_specs = [cast_spec(*_cast_tiling(w.shape, n_steps)) for w in cast]
    cast_views = [w.reshape(-1, w.shape[-1]) for w in cast]
    out = pl.pallas_call(
        kern,
        grid=(batch, A_HEADS, n_q),
        in_specs=[
            smem,
            smem,
            pl.BlockSpec((A_TQ, A_VDIM), lambda b, h, i: (b * n_q + i, h)),
            pl.BlockSpec((seq, A_VDIM), lambda b, h, i: (b, A_HEADS + h)),
            pl.BlockSpec((n_kv, A_VDIM, A_TK), lambda b, h, i: (b, h, 0)),
            pl.BlockSpec((A_VDIM, 1), lambda b, h, i: (0, 0)),
        ] + cast_specs,
        out_specs=[pl.BlockSpec((A_TQ, A_VDIM), lambda b, h, i: (b * n_q + i, h))] + cast_specs,
        out_shape=[jax.ShapeDtypeStruct((m, A_HEADS * A_VDIM), BF16)]
        + [jax.ShapeDtypeStruct(v.shape, BF16) for v in cast_views],
        scratch_shapes=[
            pltpu.VMEM((A_VDIM, A_TQ), BF16),
            pltpu.VMEM((2 + A_TQ // A_TK, A_TK, A_TQ), F32),
            pltpu.VMEM((2, A_TK, A_TQ), F32),
            pltpu.VMEM((2, A_TK, A_TQ), F32),
            pltpu.VMEM((2, 1, A_TQ), F32),
            pltpu.VMEM((2, 1, A_TQ), F32),
            pltpu.VMEM((2, A_TK, A_TQ), BF16),
            pltpu.VMEM((2, A_TK, A_TQ), BF16),
            pltpu.VMEM((2, 1, A_TQ), F32),
            pltpu.VMEM((2, 1, A_TQ), F32),
            pltpu.VMEM((2, 1, A_TQ), F32),
            pltpu.VMEM((2, 1, A_TQ), F32),
            pltpu.VMEM((2, A_VDIM, A_TQ), F32),
        ],
        compiler_params=_params(("parallel", "parallel", "arbitrary")),
        name="diff_attention",
    )(jnp.stack([lam, bound]).astype(F32), slopes, qk, qk, vt, subln_g.reshape(A_VDIM, 1), *cast_views)
    return out[0], tuple(c.reshape(w.shape) for c, w in zip(out[1:], cast))


B_QROWS = 8
B_KROWS = 16
B_TQ = B_QROWS * GRID_W
B_TK = B_KROWS * GRID_W


def _na_bias_table(rel_bias, rows):
    kr = min(NA_ROWS, rows)
    n_groups = rows // B_QROWS
    n_heads, n_drow, n_dcol = rel_bias.shape
    c = np.arange(GRID_W)
    c_start = np.clip(c - NA_COLS // 2, 0, GRID_W - NA_COLS)
    kc = np.arange(GRID_W)
    col_ok = (kc[None, :] >= c_start[:, None]) & (kc[None, :] < c_start[:, None] + NA_COLS)
    dcol = kc[None, :] - c[:, None] + (NA_COLS - 1)
    select = (np.arange(n_dcol)[:, None, None] == dcol[None]) & col_ok[None]
    cols = jnp.einsum("hrd,dck->hcrk", rel_bias * LOG2E, jnp.asarray(select, F32),
                      precision=lax.Precision.HIGHEST)
    strip = jnp.where(col_ok[None, :, None, :], cols, NEG_BIG).reshape(n_heads, GRID_W, n_drow * GRID_W)
    row_blocks = []
    for g in (0, 1, n_groups - 1):
        k_row0 = int(np.clip(g * B_QROWS - kr // 2, 0, rows - B_KROWS))
        for r in range(g * B_QROWS, (g + 1) * B_QROWS):
            r_start = int(np.clip(r - kr // 2, 0, rows - kr))
            lead = r_start - k_row0
            d0 = r_start - r + (NA_ROWS - 1)
            parts = [jnp.full((n_heads, GRID_W, lead * GRID_W), NEG_BIG, F32),
                     strip[:, :, d0 * GRID_W:(d0 + kr) * GRID_W],
                     jnp.full((n_heads, GRID_W, (B_KROWS - kr - lead) * GRID_W), NEG_BIG, F32)]
            row_blocks.append(jnp.concatenate(parts, axis=-1))
    return jnp.stack(row_blocks, axis=1).reshape(n_heads, 3, B_TQ, B_TK)


def _na_kernel(scal_ref, q_ref, k_ref, v_ref, bias_ref, o_ref, p0_ref, p1_ref, l0_ref, l1_ref, *, rows):
    n_groups = rows // B_QROWS
    shift, spread = scal_ref[0], scal_ref[1]

    def key_start(g):
        k_row0 = jnp.clip(g * B_QROWS - NA_ROWS // 2, 0, rows - B_KROWS)
        return pl.multiple_of(k_row0 * GRID_W, (NA_ROWS // 2) * GRID_W)

    def probabilities(g, p_ref, l_ref, fixed_shift):
        q0 = pl.multiple_of(g * B_TQ, B_TQ)
        variant = jnp.where(g == 0, 0, jnp.where(g == n_groups - 1, 2, 1))
        s = _dot_nt(q_ref[pl.ds(q0, B_TQ), :], k_ref[pl.ds(key_start(g), B_TK), :]) + bias_ref[0, variant]
        p = jnp.exp2(s - (shift if fixed_shift else jnp.max(s, axis=-1, keepdims=True)))
        l_ref[...] = jnp.sum(p, axis=-1, keepdims=True)
        p_ref[...] = p.astype(BF16)

    def values(g, p_ref, l_ref):
        q0 = pl.multiple_of(g * B_TQ, B_TQ)
        o = _dot(p_ref[...], v_ref[pl.ds(key_start(g), B_TK), :])
        o_ref[pl.ds(q0, B_TQ), :] = (o * (1.0 / l_ref[...])).astype(BF16)

    def pipeline(fixed_shift):
        probabilities(0, p0_ref, l0_ref, fixed_shift)

        def pair(gg, carry):
            g = 2 * gg
            probabilities(g + 1, p1_ref, l1_ref, fixed_shift)
            values(g, p0_ref, l0_ref)
            probabilities(jnp.minimum(g + 2, n_groups - 1), p0_ref, l0_ref, fixed_shift)
            values(g + 1, p1_ref, l1_ref)
            return carry

        lax.fori_loop(0, n_groups // 2, pair, 0)

    fixed_shift_ok = spread <= 2.0 * A_MAX_FIXED_SHIFT
    pl.when(fixed_shift_ok)(lambda: pipeline(True))
    pl.when(jnp.logical_not(fixed_shift_ok))(lambda: pipeline(False))


def _neighbourhood_attention(qkv, bias, shift, spread, batch, seq):
    m = qkv.shape[0]
    rows = seq // GRID_W
    assert rows % (2 * B_QROWS) == 0 and rows >= B_KROWS and NA_ROWS <= rows
    kern = functools.partial(_na_kernel, rows=rows)
    return pl.pallas_call(
        kern,
        grid=(batch, B_HEADS),
        in_specs=[
            pl.BlockSpec(memory_space=pltpu.SMEM),
            pl.BlockSpec((seq, HEAD_DIM), lambda b, h: (b, h)),
            pl.BlockSpec((seq, HEAD_DIM), lambda b, h: (b, B_HEADS + h)),
            pl.BlockSpec((seq, HEAD_DIM), lambda b, h: (b, 2 * B_HEADS + h)),
            pl.BlockSpec((1, 3, B_TQ, B_TK), lambda b, h: (h, 0, 0, 0)),
        ],
        out_specs=pl.BlockSpec((seq, HEAD_DIM), lambda b, h: (b, h)),
        out_shape=jax.ShapeDtypeStruct((m, B_HEADS * HEAD_DIM), BF16),
        scratch_shapes=[
            pltpu.VMEM((B_TQ, B_TK), BF16),
            pltpu.VMEM((B_TQ, B_TK), BF16),
            pltpu.VMEM((B_TQ, 1), F32),
            pltpu.VMEM((B_TQ, 1), F32),
        ],
        compiler_params=_params(("parallel", "arbitrary")),
        name="neighbourhood_attention",
    )(jnp.stack([shift, spread]).astype(F32), qkv, qkv, qkv, bias)


C_TQ = 256
C_TK = C_TQ + 2 * C_WINDOW
C_LEADS = 3


def _swa_kernel(scal_ref, slope_ref, sink_ref, q_ref, k_ref, v_ref, o_ref,
                tbl_ref, p0_ref, p1_ref, l0_ref, l1_ref):
    kv = pl.program_id(1)
    seq = k_ref.shape[0]
    n_tiles = seq // C_TQ
    bound, spread = scal_ref[0], scal_ref[1]

    rows = lax.broadcasted_iota(jnp.int32, (C_TQ, C_TK), 0)
    cols = lax.broadcasted_iota(jnp.int32, (C_TQ, C_TK), 1)
    for lead in range(C_LEADS):
        dist = jnp.abs(rows - cols + lead * C_WINDOW)
        tbl_ref[lead] = jnp.where(dist <= C_WINDOW, -dist.astype(F32), NEG_BIG)

    def key_start(t0):
        return pl.multiple_of(jnp.clip(t0 - C_WINDOW, 0, seq - C_TK), C_WINDOW)

    def probabilities(i, p_ref, l_ref, fixed_shift):
        t0 = pl.multiple_of(i * C_TQ, C_TQ)
        k0 = key_start(t0)
        lead = (t0 - k0) // C_WINDOW
        kb = k_ref[pl.ds(k0, C_TK), :]
        for g in range(C_GROUP):
            cols_g = slice(g * HEAD_DIM, (g + 1) * HEAD_DIM)
            slope = slope_ref[kv * C_GROUP + g] * LOG2E
            s = _dot_nt(q_ref[pl.ds(t0, C_TQ), cols_g], kb) + slope * tbl_ref[lead]
            sink = sink_ref[kv * C_GROUP + g] * LOG2E
            if fixed_shift:
                m = jnp.maximum(bound, sink)
            else:
                m = jnp.maximum(jnp.max(s, axis=-1, keepdims=True), sink)
            p = jnp.exp2(s - m)
            l_ref[g] = jnp.sum(p, axis=-1, keepdims=True) + jnp.exp2(sink - m)
            p_ref[g] = p.astype(BF16)

    def values(i, p_ref, l_ref):
        t0 = pl.multiple_of(i * C_TQ, C_TQ)
        vb = v_ref[pl.ds(key_start(t0), C_TK), :]
        for g in range(C_GROUP):
            cols_g = slice(g * HEAD_DIM, (g + 1) * HEAD_DIM)
            o_ref[pl.ds(t0, C_TQ), cols_g] = (_dot(p_ref[g], vb) * (1.0 / l_ref[g])).astype(BF16)

    def pipeline(fixed_shift):
        probabilities(0, p0_ref, l0_ref, fixed_shift)

        def pair(ii, carry):
            i = 2 * ii
            probabilities(i + 1, p1_ref, l1_ref, fixed_shift)
            values(i, p0_ref, l0_ref)
            probabilities(jnp.minimum(i + 2, n_tiles - 1), p0_ref, l0_ref, fixed_shift)
            values(i + 1, p1_ref, l1_ref)
            return carry

        lax.fori_loop(0, n_tiles // 2, pair, 0)

    fixed_shift_ok = spread <= 2.0 * A_MAX_FIXED_SHIFT
    pl.when(fixed_shift_ok)(lambda: pipeline(True))
    pl.when(jnp.logical_not(fixed_shift_ok))(lambda: pipeline(False))


def _window_attention(qkv, sink, bound, batch, seq):
    m = qkv.shape[0]
    assert seq % C_TQ == 0 and seq >= C_TK and C_TQ == (C_LEADS - 1) * C_WINDOW
    sink_max = jnp.max(sink.astype(F32)) * LOG2E
    scal = jnp.stack([bound, bound + jnp.maximum(bound, sink_max)]).astype(F32)
    slopes = jnp.asarray(2.0 ** (-8.0 * np.arange(1, C_HEADS + 1) / C_HEADS), F32)
    gw = C_GROUP * HEAD_DIM
    smem = pl.BlockSpec(memory_space=pltpu.SMEM)
    return pl.pallas_call(
        _swa_kernel,
        grid=(batch, C_KV_HEADS),
        in_specs=[
            smem,
            smem,
            smem,
            pl.BlockSpec((seq, gw), lambda b, kv: (b, kv)),
            pl.BlockSpec((seq, HEAD_DIM), lambda b, kv: (b, C_HEADS + kv)),
            pl.BlockSpec((seq, HEAD_DIM), lambda b, kv: (b, C_HEADS + C_KV_HEADS + kv)),
        ],
        out_specs=pl.BlockSpec((seq, gw), lambda b, kv: (b, kv)),
        out_shape=jax.ShapeDtypeStruct((m, C_HEADS * HEAD_DIM), BF16),
        scratch_shapes=[
            pltpu.VMEM((C_LEADS, C_TQ, C_TK), F32),
            pltpu.VMEM((C_GROUP, C_TQ, C_TK), BF16),
            pltpu.VMEM((C_GROUP, C_TQ, C_TK), BF16),
            pltpu.VMEM((C_GROUP, C_TQ, 1), F32),
            pltpu.VMEM((C_GROUP, C_TQ, 1), F32),
        ],
        compiler_params=_params(("parallel", "arbitrary")),
        name="window_attention",
    )(scal, slopes, sink.astype(F32), qkv, qkv, qkv)


def _col_gain(q_gain, k_gain, n_q_heads, n_k_heads, n_cols):
    q_scale = HEAD_DIM ** -0.5 * LOG2E
    parts = [jnp.tile(q_gain.astype(F32) * q_scale, n_q_heads), jnp.tile(k_gain.astype(F32), n_k_heads)]
    gain = jnp.concatenate(parts)
    return jnp.pad(gain, (0, n_cols - gain.shape[0]), constant_values=1.0).reshape(1, n_cols)


def kernel(x, norm_ffn1, ffn1_w_gate, ffn1_w_up, ffn1_w_down, norm_mix, norm_ffn2, ffn2_w_gate, ffn2_w_up, ffn2_w_down, a_w_qkv, a_q_norm, a_k_norm, a_lambda_q1, a_lambda_k1, a_lambda_q2, a_lambda_k2, a_subln, a_w_o, b_w_qkv, b_q_norm, b_k_norm, b_rel_bias, b_w_o, c_w_qkv, c_q_norm, c_k_norm, c_sink, c_w_o):
    batch, seq, d = x.shape
    x = x.reshape(batch * seq, d)
    first_slab = lambda w: w[:1].astype(BF16)
    ffn1 = tuple(map(first_slab, (ffn1_w_gate, ffn1_w_up, ffn1_w_down)))
    ffn2 = None
    later_f32 = (ffn1_w_gate, ffn1_w_up, ffn1_w_down, ffn2_w_gate, ffn2_w_up, ffn2_w_down,
                 a_w_qkv, a_w_o, b_w_qkv, b_w_o, c_w_qkv, c_w_o)
    a_w_qkv, a_w_o = first_slab(a_w_qkv), first_slab(a_w_o)

    for i in range(DEPTH):
        x = _ffn(x, norm_ffn1[i], *ffn1, i)
        kind, j = i % N_MIXERS, i // N_MIXERS
        if kind == 0:
            lambda_init = 0.8 - 0.6 * math.exp(-0.3 * i)
            lam = (jnp.exp(jnp.sum(a_lambda_q1[j].astype(F32) * a_lambda_k1[j].astype(F32)))
                   - jnp.exp(jnp.sum(a_lambda_q2[j].astype(F32) * a_lambda_k2[j].astype(F32))) + lambda_init)
            n_qk = 2 * A_HEADS * 2 * HEAD_DIM
            gain = _col_gain(a_q_norm[j], a_k_norm[j], 2 * A_HEADS, 2 * A_HEADS, n_qk)
            qk = _qkv(x, norm_mix[i], a_w_qkv, j, gain, n_qk, n=n_qk, sub_blocks=QKV_SUB_BLOCKS)
            vt = _proj_t(x, norm_mix[i], a_w_qkv, j, n_qk)
            bound = _logit_bound(a_q_norm[j], a_k_norm[j])
            o, cast = _diff_attention(qk, vt, lam, bound, a_subln[j], batch, seq, lambda_init,
                                      cast=later_f32 if i == 0 else ())
            x = _out_proj(o, a_w_o, j, x)
            if i == 0:
                ffn1, ffn2 = cast[:3], cast[3:6]
                a_w_qkv, a_w_o, b_w_qkv, b_w_o, c_w_qkv, c_w_o = cast[6:]
        elif kind == 1:
            n_qk = 2 * B_HEADS * HEAD_DIM
            gain = _col_gain(b_q_norm[j], b_k_norm[j], B_HEADS, B_HEADS, b_w_qkv.shape[2])
            qkv = _qkv(x, norm_mix[i], b_w_qkv, j, gain, n_qk, sub_blocks=QKV_SUB_BLOCKS)
            rel_bias = b_rel_bias[j].astype(F32)
            bias = _na_bias_table(rel_bias, seq // GRID_W)
            bound = _logit_bound(b_q_norm[j], b_k_norm[j])
            bias_max, bias_min = jnp.max(rel_bias) * LOG2E, jnp.min(rel_bias) * LOG2E
            o = _neighbourhood_attention(qkv, bias, bound + bias_max, 2.0 * bound + bias_max - bias_min,
                                         batch, seq)
            x = _out_proj(o, b_w_o, j, x)
        else:
            n_qk = (C_HEADS + C_KV_HEADS) * HEAD_DIM
            gain = _col_gain(c_q_norm[j], c_k_norm[j], C_HEADS, C_KV_HEADS, c_w_qkv.shape[2])
            qkv = _qkv(x, norm_mix[i], c_w_qkv, j, gain, n_qk)
            o = _window_attention(qkv, c_sink[j], _logit_bound(c_q_norm[j], c_k_norm[j]), batch, seq)
            x = _out_proj(o, c_w_o, j, x)
        x = _ffn(x, norm_ffn2[i], *ffn2, i)
    return x.reshape(batch, seq, d)
```

```python
import functools
import math

import jax
import jax.numpy as jnp
import numpy as np
from jax import lax
from jax.experimental import pallas as pl
from jax.experimental.pallas import tpu as pltpu

D_MODEL = 2048
DEPTH = 4
N_MIXERS = 3
HEAD_DIM = 128
RMS_EPS = 1e-6
A_HEADS = D_MODEL // (2 * HEAD_DIM)
A_VDIM = 2 * HEAD_DIM
B_HEADS = D_MODEL // HEAD_DIM
GRID_W = 64
NA_ROWS = 8
NA_COLS = 16
C_HEADS = D_MODEL // HEAD_DIM
C_KV_HEADS = 4
C_GROUP = C_HEADS // C_KV_HEADS
C_WINDOW = 128

LOG2E = math.log2(math.e)
NEG_BIG = -1e30
BF16 = jnp.bfloat16
F32 = jnp.float32

V7X_VMEM_BYTES = 64 * 1024 * 1024
VMEM_LIMIT = V7X_VMEM_BYTES - 8 * 1024 * 1024
LANE = 128
SUBLANE = 8


def _params(semantics):
    return pltpu.CompilerParams(dimension_semantics=semantics, vmem_limit_bytes=VMEM_LIMIT)


def _dot(a, b):
    return jnp.dot(a, b, preferred_element_type=F32)


def _dot_nt(a, b):
    return lax.dot_general(a, b, (((1,), (1,)), ((), ())), preferred_element_type=F32)


def _rms_normalise(x):
    return x * lax.rsqrt(jnp.mean(x * x, axis=-1, keepdims=True) + RMS_EPS)


FFN_TM = 1024
FFN_TF = 512


def _ffn_kernel(x_ref, g_ref, wg_ref, wu_ref, wd_ref, o_ref, h_ref):
    j = pl.program_id(1)

    @pl.when(j == 0)
    def _():
        x = x_ref[...]
        h_ref[...] = (_rms_normalise(x) * g_ref[...]).astype(BF16)
        o_ref[...] = x

    h = h_ref[...]
    gate = _dot(h, wg_ref[...])
    up = _dot(h, wu_ref[...])
    act = gate * (0.5 / (1.0 + jnp.exp(-gate))) * up
    o_ref[...] += _dot(act.astype(BF16), wd_ref[...])


def _ffn(x, g, wg, wu, wd, layer):
    m, d = x.shape
    f = wg.shape[2]
    return pl.pallas_call(
        _ffn_kernel,
        grid=(m // FFN_TM, f // FFN_TF),
        in_specs=[
            pl.BlockSpec((FFN_TM, d), lambda i, j: (i, 0)),
            pl.BlockSpec((1, d), lambda i, j: (0, 0)),
            pl.BlockSpec((None, d, FFN_TF), lambda i, j: (layer, 0, j)),
            pl.BlockSpec((None, d, FFN_TF), lambda i, j: (layer, 0, j)),
            pl.BlockSpec((None, FFN_TF, d), lambda i, j: (layer, j, 0)),
        ],
        out_specs=pl.BlockSpec((FFN_TM, d), lambda i, j: (i, 0)),
        out_shape=jax.ShapeDtypeStruct((m, d), F32),
        scratch_shapes=[pltpu.VMEM((FFN_TM, d), BF16)],
        compiler_params=_params(("parallel", "arbitrary")),
        name="macaron_ffn",
    )(x, g.reshape(1, d), wg, wu, wd)


PROJ_TM = 1024
PROJ_TN = 512
QKV_SUB_BLOCKS = 4


def _qkv_kernel(x_ref, g_ref, w_ref, cg_ref, o_ref, h_ref, *, n_norm_steps, sub_blocks):
    j = pl.program_id(1)

    @pl.when(j == 0)
    def _():
        h_ref[...] = (_rms_normalise(x_ref[...]) * g_ref[...]).astype(BF16)

    def project(blk):
        cols = slice(blk * PROJ_TN, (blk + 1) * PROJ_TN)
        return _dot(h_ref[...], w_ref[:, cols])

    @pl.when(j < n_norm_steps)
    def _():
        for blk in range(sub_blocks):
            y = project(blk)
            for c in range(PROJ_TN // HEAD_DIM):
                cols = slice(blk * PROJ_TN + c * HEAD_DIM, blk * PROJ_TN + (c + 1) * HEAD_DIM)
                head = slice(c * HEAD_DIM, (c + 1) * HEAD_DIM)
                o_ref[:, cols] = (_rms_normalise(y[:, head]) * cg_ref[:, cols]).astype(BF16)

    @pl.when(j >= n_norm_steps)
    def _():
        for blk in range(sub_blocks):
            o_ref[:, blk * PROJ_TN:(blk + 1) * PROJ_TN] = project(blk).astype(BF16)


def _qkv(x, g, w, layer, col_gain, n_norm_cols, n=None, sub_blocks=1):
    m, d = x.shape
    n = w.shape[2] if n is None else n
    tn = sub_blocks * PROJ_TN
    assert n % tn == 0 and n_norm_cols % tn == 0
    kern = functools.partial(_qkv_kernel, n_norm_steps=n_norm_cols // tn, sub_blocks=sub_blocks)
    return pl.pallas_call(
        kern,
        grid=(m // PROJ_TM, n // tn),
        in_specs=[
            pl.BlockSpec((PROJ_TM, d), lambda i, j: (i, 0)),
            pl.BlockSpec((1, d), lambda i, j: (0, 0)),
            pl.BlockSpec((None, d, tn), lambda i, j: (layer, 0, j)),
            pl.BlockSpec((1, tn), lambda i, j: (0, j)),
        ],
        out_specs=pl.BlockSpec((PROJ_TM, tn), lambda i, j: (i, j)),
        out_shape=jax.ShapeDtypeStruct((m, n), BF16),
        scratch_shapes=[pltpu.VMEM((PROJ_TM, d), BF16)],
        compiler_params=_params(("parallel", "arbitrary")),
        name="mixer_qkv",
    )(x, g.reshape(1, d), w, col_gain)


A_TK = 512


def _proj_t_kernel(x_ref, g_ref, w_ref, o_ref, h_ref):
    @pl.when(pl.program_id(1) == 0)
    def _():
        h_ref[...] = (_rms_normalise(x_ref[...]) * g_ref[...]).astype(BF16)

    yt = _dot(h_ref[...], w_ref[...]).T.astype(BF16)
    for r in range(PROJ_TM // A_TK):
        o_ref[r] = yt[:, r * A_TK:(r + 1) * A_TK]


def _proj_t(x, g, w, layer, col0):
    m, d = x.shape
    n = w.shape[2] - col0
    j0 = col0 // PROJ_TN
    return pl.pallas_call(
        _proj_t_kernel,
        grid=(m // PROJ_TM, n // PROJ_TN),
        in_specs=[
            pl.BlockSpec((PROJ_TM, d), lambda i, j: (i, 0)),
            pl.BlockSpec((1, d), lambda i, j: (0, 0)),
            pl.BlockSpec((None, d, PROJ_TN), lambda i, j: (layer, 0, j0 + j)),
        ],
        out_specs=pl.BlockSpec((PROJ_TM // A_TK, PROJ_TN, A_TK), lambda i, j: (i, j, 0)),
        out_shape=jax.ShapeDtypeStruct((m // A_TK, n, A_TK), BF16),
        scratch_shapes=[pltpu.VMEM((PROJ_TM, d), BF16)],
        compiler_params=_params(("parallel", "arbitrary")),
        name="mixer_v_transposed",
    )(x, g.reshape(1, d), w)


def _out_proj_kernel(a_ref, w_ref, x_ref, o_ref):
    o_ref[...] = x_ref[...] + _dot(a_ref[...], w_ref[...])


OUT_TM = 512


def _out_proj(a, w, layer, x):
    m, k = a.shape
    n = w.shape[2]
    return pl.pallas_call(
        _out_proj_kernel,
        grid=(m // OUT_TM,),
        in_specs=[
            pl.BlockSpec((OUT_TM, k), lambda i: (i, 0)),
            pl.BlockSpec((None, k, n), lambda i: (layer, 0, 0)),
            pl.BlockSpec((OUT_TM, n), lambda i: (i, 0)),
        ],
        out_specs=pl.BlockSpec((OUT_TM, n), lambda i: (i, 0)),
        out_shape=jax.ShapeDtypeStruct((m, n), F32),
        compiler_params=_params(("parallel",)),
        name="mixer_out_proj",
    )(a, w, x)


A_TQ = 512
A_PAIRS_PER_TRIP = 4
A_MAX_FIXED_SHIFT = 40.0


def _diff_attn_kernel(scal_ref, slope_ref, q_ref, k_ref, vt_ref, sg_ref, *refs, out_scale, n_cast):
    cast_src, o_ref, cast_dst = refs[:n_cast], refs[n_cast], refs[n_cast + 1:2 * n_cast + 1]
    (qt_ref, tbl_ref, s0_ref, s1_ref, x0_ref, x1_ref, p0_ref, p1_ref, a0_ref, a1_ref,
     m_ref, l_ref, acc_ref) = refs[2 * n_cast + 1:]
    for src_ref, dst_ref in zip(cast_src, cast_dst):
        dst_ref[...] = src_ref[...].astype(BF16)

    h = pl.program_id(1)
    qi = pl.program_id(2)
    n_kv = vt_ref.shape[0]
    kv_per_q = A_TQ // A_TK
    c = slope_ref[h] * LOG2E

    @pl.when(qi == 0)
    def _():
        rows = lax.broadcasted_iota(jnp.int32, (A_TK, A_TQ), 0)
        cols = lax.broadcasted_iota(jnp.int32, (A_TK, A_TQ), 1)
        cd = c * (rows - cols).astype(F32)
        tbl_ref[0] = cd
        tbl_ref[1] = -cd
        for d in range(kv_per_q):
            tbl_ref[2 + d] = -jnp.abs(cd + c * float(d * A_TK))

    qt_ref[...] = q_ref[...].astype(F32).T.astype(BF16)
    m_ref[...] = jnp.full(m_ref.shape, NEG_BIG, F32)
    l_ref[...] = jnp.zeros(l_ref.shape, F32)
    acc_ref[...] = jnp.zeros(acc_ref.shape, F32)
    p1_ref[...] = jnp.zeros(p1_ref.shape, BF16)
    a1_ref[...] = jnp.ones(a1_ref.shape, F32)

    def bias_of(j):
        d = j - qi * kv_per_q
        off = c * (d * A_TK).astype(F32)
        idx = jnp.where(d < 0, 0, jnp.where(d >= kv_per_q, 1, 2 + d))
        kappa = jnp.where(d < 0, off, jnp.where(d >= kv_per_q, -off, 0.0))
        return idx, kappa

    def biased_logits(j, half, idx):
        k0 = pl.multiple_of(j * A_TK, A_TK)
        cols_h = slice(half * HEAD_DIM, (half + 1) * HEAD_DIM)
        return _dot(k_ref[pl.ds(k0, A_TK), cols_h], qt_ref[cols_h, :]) + tbl_ref[idx]

    bound = scal_ref[1]
    fixed_shift_ok = bound <= A_MAX_FIXED_SHIFT

    @pl.when(fixed_shift_ok)
    def _():
        def probabilities(j, p_ref, live):
            idx, kappa = bias_of(j)
            shift = kappa - bound
            for half in range(2):
                p = jnp.exp2(biased_logits(j, half, idx) + shift)
                l_ref[half] += live * jnp.sum(p, axis=0, keepdims=True)
                p_ref[half] = p.astype(BF16)

        def values(j, p_ref):
            vtb = vt_ref[j]
            for half in range(2):
                acc_ref[half] += _dot(vtb, p_ref[half])

        probabilities(0, p0_ref, 1.0)

        def pair(j):
            probabilities(j + 1, p1_ref, 1.0)
            values(j, p0_ref)
            nxt = j + 2
            probabilities(jnp.minimum(nxt, n_kv - 1), p0_ref, (nxt < n_kv).astype(F32))
            values(j + 1, p1_ref)

        def pairs(jj, carry):
            for u in range(A_PAIRS_PER_TRIP):
                pair(2 * (A_PAIRS_PER_TRIP * jj + u))
            return carry

        lax.fori_loop(0, n_kv // (2 * A_PAIRS_PER_TRIP), pairs, 0)

    @pl.when(jnp.logical_not(fixed_shift_ok))
    def _():
        def logits(j, z_ref, zmax_ref):
            idx, _ = bias_of(j)
            for half in range(2):
                z = biased_logits(j, half, idx)
                z_ref[half] = z
                zmax_ref[half] = jnp.max(z, axis=0, keepdims=True)

        def softmax(j, z_ref, zmax_ref, p_ref, a_ref):
            _, kappa = bias_of(j)
            for half in range(2):
                m_old = m_ref[half]
                m_new = jnp.maximum(m_old, zmax_ref[half] + kappa)
                p = jnp.exp2(z_ref[half] - (m_new - kappa))
                alpha = jnp.exp2(m_old - m_new)
                l_ref[half] = alpha * l_ref[half] + jnp.sum(p, axis=0, keepdims=True)
                m_ref[half] = m_new
                a_ref[half] = alpha
                p_ref[half] = p.astype(BF16)

        def values(j, p_ref, a_ref):
            vtb = vt_ref[j]
            for half in range(2):
                acc_ref[half] = a_ref[half] * acc_ref[half] + _dot(vtb, p_ref[half])

        logits(0, s0_ref, x0_ref)

        def pair(jj, carry):
            j = 2 * jj
            logits(j + 1, s1_ref, x1_ref)
            softmax(j, s0_ref, x0_ref, p0_ref, a0_ref)
            values(jnp.maximum(j - 1, 0), p1_ref, a1_ref)
            logits(jnp.minimum(j + 2, n_kv - 1), s0_ref, x0_ref)
            softmax(j + 1, s1_ref, x1_ref, p1_ref, a1_ref)
            values(j, p0_ref, a0_ref)
            return carry

        lax.fori_loop(0, n_kv // 2, pair, 0)
        values(n_kv - 1, p1_ref, a1_ref)

    ot = acc_ref[0] * (1.0 / l_ref[0]) - scal_ref[0] * (acc_ref[1] * (1.0 / l_ref[1]))
    inv_rms = lax.rsqrt(jnp.mean(ot * ot, axis=0, keepdims=True) + RMS_EPS)
    ot = ot * inv_rms * (sg_ref[...] * out_scale)
    o_ref[...] = ot.T.astype(BF16)


def _logit_bound(q_gain, k_gain):
    q_scale = HEAD_DIM ** -0.5 * LOG2E
    return (HEAD_DIM * q_scale * 1.02) * jnp.max(jnp.abs(q_gain.astype(F32))) * jnp.max(jnp.abs(k_gain.astype(F32)))


def _cast_tiling(shape, n_steps):
    n_rows, width = math.prod(shape[:-1]), shape[-1]
    steps_per_tile = 1
    while n_rows * steps_per_tile % (n_steps * 2 * SUBLANE) != 0:
        steps_per_tile *= 2
    assert n_steps % steps_per_tile == 0 and width % LANE == 0
    return (n_rows * steps_per_tile // n_steps, width), steps_per_tile


def _diff_attention(qk, vt, lam, bound, subln_g, batch, seq, lambda_init, cast=()):
    assert A_TQ % A_TK == 0 and seq % A_TQ == 0
    assert (seq // A_TK) % (2 * A_PAIRS_PER_TRIP) == 0
    m = qk.shape[0]
    n_q = seq // A_TQ
    n_kv = seq // A_TK
    n_steps = batch * A_HEADS * n_q
    slopes = jnp.asarray(2.0 ** (-8.0 * np.arange(1, A_HEADS + 1) / A_HEADS), F32)
    kern = functools.partial(_diff_attn_kernel, out_scale=1.0 - lambda_init, n_cast=len(cast))
    smem = pl.BlockSpec(memory_space=pltpu.SMEM)
    def cast_spec(tile, steps_per_tile):
        return pl.BlockSpec(tile, lambda b, h, i: (((b * A_HEADS + h) * n_q + i) // steps_per_tile, 0))

    cast_specs = [cast_spec(*_cast_tiling(w.shape, n_steps)) for w in cast]
    cast_views = [w.reshape(-1, w.shape[-1]) for w in cast]
    out = pl.pallas_call(
        kern,
        grid=(batch, A_HEADS, n_q),
        in_specs=[
            smem,
            smem,
            pl.BlockSpec((A_TQ, A_VDIM), lambda b, h, i: (b * n_q + i, h)),
            pl.BlockSpec((seq, A_VDIM), lambda b, h, i: (b, A_HEADS + h)),
            pl.BlockSpec((n_kv, A_VDIM, A_TK), lambda b, h, i: (b, h, 0)),
            pl.BlockSpec((A_VDIM, 1), lambda b, h, i: (0, 0)),
        ] + cast_specs,
        out_specs=[pl.BlockSpec((A_TQ, A_VDIM), lambda b, h, i: (b * n_q + i, h))] + cast_specs,
        out_shape=[jax.ShapeDtypeStruct((m, A_HEADS * A_VDIM), BF16)]
        + [jax.ShapeDtypeStruct(v.shape, BF16) for v in cast_views],
        scratch_shapes=[
            pltpu.VMEM((A_VDIM, A_TQ), BF16),
            pltpu.VMEM((2 + A_TQ // A_TK, A_TK, A_TQ), F32),
            pltpu.VMEM((2, A_TK, A_TQ), F32),
            pltpu.VMEM((2, A_TK, A_TQ), F32),
            pltpu.VMEM((2, 1, A_TQ), F32),
            pltpu.VMEM((2, 1, A_TQ), F32),
            pltpu.VMEM((2, A_TK, A_TQ), BF16),
            pltpu.VMEM((2, A_TK, A_TQ), BF16),
            pltpu.VMEM((2, 1, A_TQ), F32),
            pltpu.VMEM((2, 1, A_TQ), F32),
            pltpu.VMEM((2, 1, A_TQ), F32),
            pltpu.VMEM((2, 1, A_TQ), F32),
            pltpu.VMEM((2, A_VDIM, A_TQ), F32),
        ],
        compiler_params=_params(("parallel", "parallel", "arbitrary")),
        name="diff_attention",
    )(jnp.stack([lam, bound]).astype(F32), slopes, qk, qk, vt, subln_g.reshape(A_VDIM, 1), *cast_views)
    return out[0], tuple(c.reshape(w.shape) for c, w in zip(out[1:], cast))


B_QROWS = 8
B_KROWS = 16
B_TQ = B_QROWS * GRID_W
B_TK = B_KROWS * GRID_W


def _na_bias_table(rel_bias, rows):
    kr = min(NA_ROWS, rows)
    n_groups = rows // B_QROWS
    n_heads, n_drow, n_dcol = rel_bias.shape
    c = np.arange(GRID_W)
    c_start = np.clip(c - NA_COLS // 2, 0, GRID_W - NA_COLS)
    kc = np.arange(GRID_W)
    col_ok = (kc[None, :] >= c_start[:, None]) & (kc[None, :] < c_start[:, None] + NA_COLS)
    dcol = kc[None, :] - c[:, None] + (NA_COLS - 1)
    select = (np.arange(n_dcol)[:, None, None] == dcol[None]) & col_ok[None]
    cols = jnp.einsum("hrd,dck->hcrk", rel_bias * LOG2E, jnp.asarray(select, F32),
                      precision=lax.Precision.HIGHEST)
    strip = jnp.where(col_ok[None, :, None, :], cols, NEG_BIG).reshape(n_heads, GRID_W, n_drow * GRID_W)
    row_blocks = []
    for g in (0, 1, n_groups - 1):
        k_row0 = int(np.clip(g * B_QROWS - kr // 2, 0, rows - B_KROWS))
        for r in range(g * B_QROWS, (g + 1) * B_QROWS):
            r_start = int(np.clip(r - kr // 2, 0, rows - kr))
            lead = r_start - k_row0
            d0 = r_start - r + (NA_ROWS - 1)
            parts = [jnp.full((n_heads, GRID_W, lead * GRID_W), NEG_BIG, F32),
                     strip[:, :, d0 * GRID_W:(d0 + kr) * GRID_W],
                     jnp.full((n_heads, GRID_W, (B_KROWS - kr - lead) * GRID_W), NEG_BIG, F32)]
            row_blocks.append(jnp.concatenate(parts, axis=-1))
    return jnp.stack(row_blocks, axis=1).reshape(n_heads, 3, B_TQ, B_TK)


def _na_kernel(scal_ref, q_ref, k_ref, v_ref, bias_ref, o_ref, p0_ref, p1_ref, l0_ref, l1_ref, *, rows):
    n_groups = rows // B_QROWS
    shift, spread = scal_ref[0], scal_ref[1]

    def key_start(g):
        k_row0 = jnp.clip(g * B_QROWS - NA_ROWS // 2, 0, rows - B_KROWS)
        return pl.multiple_of(k_row0 * GRID_W, (NA_ROWS // 2) * GRID_W)

    def probabilities(g, p_ref, l_ref, fixed_shift):
        q0 = pl.multiple_of(g * B_TQ, B_TQ)
        variant = jnp.where(g == 0, 0, jnp.where(g == n_groups - 1, 2, 1))
        s = _dot_nt(q_ref[pl.ds(q0, B_TQ), :], k_ref[pl.ds(key_start(g), B_TK), :]) + bias_ref[0, variant]
        p = jnp.exp2(s - (shift if fixed_shift else jnp.max(s, axis=-1, keepdims=True)))
        l_ref[...] = jnp.sum(p, axis=-1, keepdims=True)
        p_ref[...] = p.astype(BF16)

    def values(g, p_ref, l_ref):
        q0 = pl.multiple_of(g * B_TQ, B_TQ)
        o = _dot(p_ref[...], v_ref[pl.ds(key_start(g), B_TK), :])
        o_ref[pl.ds(q0, B_TQ), :] = (o * (1.0 / l_ref[...])).astype(BF16)

    def pipeline(fixed_shift):
        probabilities(0, p0_ref, l0_ref, fixed_shift)

        def pair(gg, carry):
            g = 2 * gg
            probabilities(g + 1, p1_ref, l1_ref, fixed_shift)
            values(g, p0_ref, l0_ref)
            probabilities(jnp.minimum(g + 2, n_groups - 1), p0_ref, l0_ref, fixed_shift)
            values(g + 1, p1_ref, l1_ref)
            return carry

        lax.fori_loop(0, n_groups // 2, pair, 0)

    fixed_shift_ok = spread <= 2.0 * A_MAX_FIXED_SHIFT
    pl.when(fixed_shift_ok)(lambda: pipeline(True))
    pl.when(jnp.logical_not(fixed_shift_ok))(lambda: pipeline(False))


def _neighbourhood_attention(qkv, bias, shift, spread, batch, seq):
    m = qkv.shape[0]
    rows = seq // GRID_W
    assert rows % (2 * B_QROWS) == 0 and rows >= B_KROWS and NA_ROWS <= rows
    kern = functools.partial(_na_kernel, rows=rows)
    return pl.pallas_call(
        kern,
        grid=(batch, B_HEADS),
        in_specs=[
            pl.BlockSpec(memory_space=pltpu.SMEM),
            pl.BlockSpec((seq, HEAD_DIM), lambda b, h: (b, h)),
            pl.BlockSpec((seq, HEAD_DIM), lambda b, h: (b, B_HEADS + h)),
            pl.BlockSpec((seq, HEAD_DIM), lambda b, h: (b, 2 * B_HEADS + h)),
            pl.BlockSpec((1, 3, B_TQ, B_TK), lambda b, h: (h, 0, 0, 0)),
        ],
        out_specs=pl.BlockSpec((seq, HEAD_DIM), lambda b, h: (b, h)),
        out_shape=jax.ShapeDtypeStruct((m, B_HEADS * HEAD_DIM), BF16),
        scratch_shapes=[
            pltpu.VMEM((B_TQ, B_TK), BF16),
            pltpu.VMEM((B_TQ, B_TK), BF16),
            pltpu.VMEM((B_TQ, 1), F32),
            pltpu.VMEM((B_TQ, 1), F32),
        ],
        compiler_params=_params(("parallel", "arbitrary")),
        name="neighbourhood_attention",
    )(jnp.stack([shift, spread]).astype(F32), qkv, qkv, qkv, bias)


C_TQ = 256
C_TK = C_TQ + 2 * C_WINDOW
C_LEADS = 3


def _swa_kernel(scal_ref, slope_ref, sink_ref, q_ref, k_ref, v_ref, o_ref,
                tbl_ref, p0_ref, p1_ref, l0_ref, l1_ref):
    kv = pl.program_id(1)
    seq = k_ref.shape[0]
    n_tiles = seq // C_TQ
    bound, spread = scal_ref[0], scal_ref[1]

    rows = lax.broadcasted_iota(jnp.int32, (C_TQ, C_TK), 0)
    cols = lax.broadcasted_iota(jnp.int32, (C_TQ, C_TK), 1)
    for lead in range(C_LEADS):
        dist = jnp.abs(rows - cols + lead * C_WINDOW)
        tbl_ref[lead] = jnp.where(dist <= C_WINDOW, -dist.astype(F32), NEG_BIG)

    def key_start(t0):
        return pl.multiple_of(jnp.clip(t0 - C_WINDOW, 0, seq - C_TK), C_WINDOW)

    def probabilities(i, p_ref, l_ref, fixed_shift):
        t0 = pl.multiple_of(i * C_TQ, C_TQ)
        k0 = key_start(t0)
        lead = (t0 - k0) // C_WINDOW
        kb = k_ref[pl.ds(k0, C_TK), :]
        for g in range(C_GROUP):
            cols_g = slice(g * HEAD_DIM, (g + 1) * HEAD_DIM)
            slope = slope_ref[kv * C_GROUP + g] * LOG2E
            s = _dot_nt(q_ref[pl.ds(t0, C_TQ), cols_g], kb) + slope * tbl_ref[lead]
            sink = sink_ref[kv * C_GROUP + g] * LOG2E
            if fixed_shift:
                m = jnp.maximum(bound, sink)
            else:
                m = jnp.maximum(jnp.max(s, axis=-1, keepdims=True), sink)
            p = jnp.exp2(s - m)
            l_ref[g] = jnp.sum(p, axis=-1, keepdims=True) + jnp.exp2(sink - m)
            p_ref[g] = p.astype(BF16)

    def values(i, p_ref, l_ref):
        t0 = pl.multiple_of(i * C_TQ, C_TQ)
        vb = v_ref[pl.ds(key_start(t0), C_TK), :]
        for g in range(C_GROUP):
            cols_g = slice(g * HEAD_DIM, (g + 1) * HEAD_DIM)
            o_ref[pl.ds(t0, C_TQ), cols_g] = (_dot(p_ref[g], vb) * (1.0 / l_ref[g])).astype(BF16)

    def pipeline(fixed_shift):
        probabilities(0, p0_ref, l0_ref, fixed_shift)

        def pair(ii, carry):
            i = 2 * ii
            probabilities(i + 1, p1_ref, l1_ref, fixed_shift)
            values(i, p0_ref, l0_ref)
            probabilities(jnp.minimum(i + 2, n_tiles - 1), p0_ref, l0_ref, fixed_shift)
            values(i + 1, p1_ref, l1_ref)
            return carry

        lax.fori_loop(0, n_tiles // 2, pair, 0)

    fixed_shift_ok = spread <= 2.0 * A_MAX_FIXED_SHIFT
    pl.when(fixed_shift_ok)(lambda: pipeline(True))
    pl.when(jnp.logical_not(fixed_shift_ok))(lambda: pipeline(False))


def _window_attention(qkv, sink, bound, batch, seq):
    m = qkv.shape[0]
    assert seq % C_TQ == 0 and seq >= C_TK and C_TQ == (C_LEADS - 1) * C_WINDOW
    sink_max = jnp.max(sink.astype(F32)) * LOG2E
    scal = jnp.stack([bound, bound + jnp.maximum(bound, sink_max)]).astype(F32)
    slopes = jnp.asarray(2.0 ** (-8.0 * np.arange(1, C_HEADS + 1) / C_HEADS), F32)
    gw = C_GROUP * HEAD_DIM
    smem = pl.BlockSpec(memory_space=pltpu.SMEM)
    return pl.pallas_call(
        _swa_kernel,
        grid=(batch, C_KV_HEADS),
        in_specs=[
            smem,
            smem,
            smem,
            pl.BlockSpec((seq, gw), lambda b, kv: (b, kv)),
            pl.BlockSpec((seq, HEAD_DIM), lambda b, kv: (b, C_HEADS + kv)),
            pl.BlockSpec((seq, HEAD_DIM), lambda b, kv: (b, C_HEADS + C_KV_HEADS + kv)),
        ],
        out_specs=pl.BlockSpec((seq, gw), lambda b, kv: (b, kv)),
        out_shape=jax.ShapeDtypeStruct((m, C_HEADS * HEAD_DIM), BF16),
        scratch_shapes=[
            pltpu.VMEM((C_LEADS, C_TQ, C_TK), F32),
            pltpu.VMEM((C_GROUP, C_TQ, C_TK), BF16),
            pltpu.VMEM((C_GROUP, C_TQ, C_TK), BF16),
            pltpu.VMEM((C_GROUP, C_TQ, 1), F32),
            pltpu.VMEM((C_GROUP, C_TQ, 1), F32),
        ],
        compiler_params=_params(("parallel", "arbitrary")),
        name="window_attention",
    )(scal, slopes, sink.astype(F32), qkv, qkv, qkv)


def _col_gain(q_gain, k_gain, n_q_heads, n_k_heads, n_cols):
    q_scale = HEAD_DIM ** -0.5 * LOG2E
    parts = [jnp.tile(q_gain.astype(F32) * q_scale, n_q_heads), jnp.tile(k_gain.astype(F32), n_k_heads)]
    gain = jnp.concatenate(parts)
    return jnp.pad(gain, (0, n_cols - gain.shape[0]), constant_values=1.0).reshape(1, n_cols)


def kernel(x, norm_ffn1, ffn1_w_gate, ffn1_w_up, ffn1_w_down, norm_mix, norm_ffn2, ffn2_w_gate, ffn2_w_up, ffn2_w_down, a_w_qkv, a_q_norm, a_k_norm, a_lambda_q1, a_lambda_k1, a_lambda_q2, a_lambda_k2, a_subln, a_w_o, b_w_qkv, b_q_norm, b_k_norm, b_rel_bias, b_w_o, c_w_qkv, c_q_norm, c_k_norm, c_sink, c_w_o):
    batch, seq, d = x.shape
    x = x.reshape(batch * seq, d)
    first_slab = lambda w: w[:1].astype(BF16)
    ffn1 = tuple(map(first_slab, (ffn1_w_gate, ffn1_w_up, ffn1_w_down)))
    ffn2 = None
    later_f32 = (ffn1_w_gate, ffn1_w_up, ffn1_w_down, ffn2_w_gate, ffn2_w_up, ffn2_w_down,
                 a_w_qkv, a_w_o, b_w_qkv, b_w_o, c_w_qkv, c_w_o)
    a_w_qkv, a_w_o = first_slab(a_w_qkv), first_slab(a_w_o)

    for i in range(DEPTH):
        x = _ffn(x, norm_ffn1[i], *ffn1, i)
        kind, j = i % N_MIXERS, i // N_MIXERS
        if kind == 0:
            lambda_init = 0.8 - 0.6 * math.exp(-0.3 * i)
            lam = (jnp.exp(jnp.sum(a_lambda_q1[j].astype(F32) * a_lambda_k1[j].astype(F32)))
                   - jnp.exp(jnp.sum(a_lambda_q2[j].astype(F32) * a_lambda_k2[j].astype(F32))) + lambda_init)
            n_qk = 2 * A_HEADS * 2 * HEAD_DIM
            gain = _col_gain(a_q_norm[j], a_k_norm[j], 2 * A_HEADS, 2 * A_HEADS, n_qk)
            qk = _qkv(x, norm_mix[i], a_w_qkv, j, gain, n_qk, n=n_qk, sub_blocks=QKV_SUB_BLOCKS)
            vt = _proj_t(x, norm_mix[i], a_w_qkv, j, n_qk)
            bound = _logit_bound(a_q_norm[j], a_k_norm[j])
            o, cast = _diff_attention(qk, vt, lam, bound, a_subln[j], batch, seq, lambda_init,
                                      cast=later_f32 if i == 0 else ())
            x = _out_proj(o, a_w_o, j, x)
            if i == 0:
                ffn1, ffn2 = cast[:3], cast[3:6]
                a_w_qkv, a_w_o, b_w_qkv, b_w_o, c_w_qkv, c_w_o = cast[6:]
        elif kind == 1:
            n_qk = 2 * B_HEADS * HEAD_DIM
            gain = _col_gain(b_q_norm[j], b_k_norm[j], B_HEADS, B_HEADS, b_w_qkv.shape[2])
            qkv = _qkv(x, norm_mix[i], b_w_qkv, j, gain, n_qk, sub_blocks=QKV_SUB_BLOCKS)
            rel_bias = b_rel_bias[j].astype(F32)
            bias = _na_bias_table(rel_bias, seq // GRID_W)
            bound = _logit_bound(b_q_norm[j], b_k_norm[j])
            bias_max, bias_min = jnp.max(rel_bias) * LOG2E, jnp.min(rel_bias) * LOG2E
            o = _neighbourhood_attention(qkv, bias, bound + bias_max, 2.0 * bound + bias_max - bias_min,
                                         batch, seq)
            x = _out_proj(o, b_w_o, j, x)
        else:
            n_qk = (C_HEADS + C_KV_HEADS) * HEAD_DIM
            gain = _col_gain(c_q_norm[j], c_k_norm[j], C_HEADS, C_KV_HEADS, c_w_qkv.shape[2])
            qkv = _qkv(x, norm_mix[i], c_w_qkv, j, gain, n_qk)
            o = _window_attention(qkv, c_sink[j], _logit_bound(c_q_norm[j], c_k_norm[j]), batch, seq)
            x = _out_proj(o, c_w_o, j, x)
        x = _ffn(x, norm_ffn2[i], *ffn2, i)
    return x.reshape(batch, seq, d)
```

```python
import functools
import math

import jax
import jax.numpy as jnp
import numpy as np
from jax import lax
from jax.experimental import pallas as pl
from jax.experimental.pallas import tpu as pltpu

D_MODEL = 2048
DEPTH = 4
N_MIXERS = 3
HEAD_DIM = 128
RMS_EPS = 1e-6
A_HEADS = D_MODEL // (2 * HEAD_DIM)
A_VDIM = 2 * HEAD_DIM
B_HEADS = D_MODEL // HEAD_DIM
GRID_W = 64
NA_ROWS = 8
NA_COLS = 16
C_HEADS = D_MODEL // HEAD_DIM
C_KV_HEADS = 4
C_GROUP = C_HEADS // C_KV_HEADS
C_WINDOW = 128

LOG2E = math.log2(math.e)
NEG_BIG = -1e30
BF16 = jnp.bfloat16
F32 = jnp.float32

V7X_VMEM_BYTES = 64 * 1024 * 1024
VMEM_LIMIT = V7X_VMEM_BYTES - 8 * 1024 * 1024
LANE = 128
SUBLANE = 8


def _params(semantics):
    return pltpu.CompilerParams(dimension_semantics=semantics, vmem_limit_bytes=VMEM_LIMIT)


def _dot(a, b):
    return jnp.dot(a, b, preferred_element_type=F32)


def _dot_nt(a, b):
    return lax.dot_general(a, b, (((1,), (1,)), ((), ())), preferred_element_type=F32)


def _rms_normalise(x):
    return x * lax.rsqrt(jnp.mean(x * x, axis=-1, keepdims=True) + RMS_EPS)


FFN_TM = 1024
FFN_TF = 512


def _ffn_kernel(x_ref, g_ref, wg_ref, wu_ref, wd_ref, o_ref, h_ref):
    j = pl.program_id(1)

    @pl.when(j == 0)
    def _():
        x = x_ref[...]
        h_ref[...] = (_rms_normalise(x) * g_ref[...]).astype(BF16)
        o_ref[...] = x

    h = h_ref[...]
    gate = _dot(h, wg_ref[...])
    up = _dot(h, wu_ref[...])
    act = gate * (0.5 / (1.0 + jnp.exp(-gate))) * up
    o_ref[...] += _dot(act.astype(BF16), wd_ref[...])


def _ffn(x, g, wg, wu, wd, layer):
    m, d = x.shape
    f = wg.shape[2]
    return pl.pallas_call(
        _ffn_kernel,
        grid=(m // FFN_TM, f // FFN_TF),
        in_specs=[
            pl.BlockSpec((FFN_TM, d), lambda i, j: (i, 0)),
            pl.BlockSpec((1, d), lambda i, j: (0, 0)),
            pl.BlockSpec((None, d, FFN_TF), lambda i, j: (layer, 0, j)),
            pl.BlockSpec((None, d, FFN_TF), lambda i, j: (layer, 0, j)),
            pl.BlockSpec((None, FFN_TF, d), lambda i, j: (layer, j, 0)),
        ],
        out_specs=pl.BlockSpec((FFN_TM, d), lambda i, j: (i, 0)),
        out_shape=jax.ShapeDtypeStruct((m, d), F32),
        scratch_shapes=[pltpu.VMEM((FFN_TM, d), BF16)],
        compiler_params=_params(("parallel", "arbitrary")),
        name="macaron_ffn",
    )(x, g.reshape(1, d), wg, wu, wd)


PROJ_TM = 1024
PROJ_TN = 512
QKV_SUB_BLOCKS = 4


def _qkv_kernel(x_ref, g_ref, w_ref, cg_ref, o_ref, h_ref, *, n_norm_steps, sub_blocks):
    j = pl.program_id(1)

    @pl.when(j == 0)
    def _():
        h_ref[...] = (_rms_normalise(x_ref[...]) * g_ref[...]).astype(BF16)

    def project(blk):
        cols = slice(blk * PROJ_TN, (blk + 1) * PROJ_TN)
        return _dot(h_ref[...], w_ref[:, cols])

    @pl.when(j < n_norm_steps)
    def _():
        for blk in range(sub_blocks):
            y = project(blk)
            for c in range(PROJ_TN // HEAD_DIM):
                cols = slice(blk * PROJ_TN + c * HEAD_DIM, blk * PROJ_TN + (c + 1) * HEAD_DIM)
                head = slice(c * HEAD_DIM, (c + 1) * HEAD_DIM)
                o_ref[:, cols] = (_rms_normalise(y[:, head]) * cg_ref[:, cols]).astype(BF16)

    @pl.when(j >= n_norm_steps)
    def _():
        for blk in range(sub_blocks):
            o_ref[:, blk * PROJ_TN:(blk + 1) * PROJ_TN] = project(blk).astype(BF16)


def _qkv(x, g, w, layer, col_gain, n_norm_cols, n=None, sub_blocks=1):
    m, d = x.shape
    n = w.shape[2] if n is None else n
    tn = sub_blocks * PROJ_TN
    assert n % tn == 0 and n_norm_cols % tn == 0
    kern = functools.partial(_qkv_kernel, n_norm_steps=n_norm_cols // tn, sub_blocks=sub_blocks)
    return pl.pallas_call(
        kern,
        grid=(m // PROJ_TM, n // tn),
        in_specs=[
            pl.BlockSpec((PROJ_TM, d), lambda i, j: (i, 0)),
            pl.BlockSpec((1, d), lambda i, j: (0, 0)),
            pl.BlockSpec((None, d, tn), lambda i, j: (layer, 0, j)),
            pl.BlockSpec((1, tn), lambda i, j: (0, j)),
        ],
        out_specs=pl.BlockSpec((PROJ_TM, tn), lambda i, j: (i, j)),
        out_shape=jax.ShapeDtypeStruct((m, n), BF16),
        scratch_shapes=[pltpu.VMEM((PROJ_TM, d), BF16)],
        compiler_params=_params(("parallel", "arbitrary")),
        name="mixer_qkv",
    )(x, g.reshape(1, d), w, col_gain)


A_TK = 512


def _proj_t_kernel(x_ref, g_ref, w_ref, o_ref, h_ref):
    @pl.when(pl.program_id(1) == 0)
    def _():
        h_ref[...] = (_rms_normalise(x_ref[...]) * g_ref[...]).astype(BF16)

    for blk in range(QKV_SUB_BLOCKS):
        cols = slice(blk * PROJ_TN, (blk + 1) * PROJ_TN)
        yt = _dot(h_ref[...], w_ref[:, cols]).T.astype(BF16)
        for r in range(PROJ_TM // A_TK):
            o_ref[r, cols, :] = yt[:, r * A_TK:(r + 1) * A_TK]


def _proj_t(x, g, w, layer, col0):
    m, d = x.shape
    n = w.shape[2] - col0
    tn = QKV_SUB_BLOCKS * PROJ_TN
    assert n % tn == 0 and col0 % tn == 0
    j0 = col0 // tn
    return pl.pallas_call(
        _proj_t_kernel,
        grid=(m // PROJ_TM, n // tn),
        in_specs=[
            pl.BlockSpec((PROJ_TM, d), lambda i, j: (i, 0)),
            pl.BlockSpec((1, d), lambda i, j: (0, 0)),
            pl.BlockSpec((None, d, tn), lambda i, j: (layer, 0, j0 + j)),
        ],
        out_specs=pl.BlockSpec((PROJ_TM // A_TK, tn, A_TK), lambda i, j: (i, j, 0)),
        out_shape=jax.ShapeDtypeStruct((m // A_TK, n, A_TK), BF16),
        scratch_shapes=[pltpu.VMEM((PROJ_TM, d), BF16)],
        compiler_params=_params(("parallel", "arbitrary")),
        name="mixer_v_transposed",
    )(x, g.reshape(1, d), w)


def _out_proj_kernel(a_ref, w_ref, x_ref, o_ref):
    o_ref[...] = x_ref[...] + _dot(a_ref[...], w_ref[...])


OUT_TM = 512


def _out_proj(a, w, layer, x):
    m, k = a.shape
    n = w.shape[2]
    return pl.pallas_call(
        _out_proj_kernel,
        grid=(m // OUT_TM,),
        in_specs=[
            pl.BlockSpec((OUT_TM, k), lambda i: (i, 0)),
            pl.BlockSpec((None, k, n), lambda i: (layer, 0, 0)),
            pl.BlockSpec((OUT_TM, n), lambda i: (i, 0)),
        ],
        out_specs=pl.BlockSpec((OUT_TM, n), lambda i: (i, 0)),
        out_shape=jax.ShapeDtypeStruct((m, n), F32),
        compiler_params=_params(("parallel",)),
        name="mixer_out_proj",
    )(a, w, x)


A_TQ = 512
A_PAIRS_PER_TRIP = 8
A_MAX_FIXED_SHIFT = 40.0


def _diff_attn_kernel(scal_ref, slope_ref, q_ref, k_ref, vt_ref, sg_ref, *refs, out_scale, n_cast):
    cast_src, o_ref, cast_dst = refs[:n_cast], refs[n_cast], refs[n_cast + 1:2 * n_cast + 1]
    (qt_ref, tbl_ref, s0_ref, s1_ref, x0_ref, x1_ref, p0_ref, p1_ref, a0_ref, a1_ref,
     m_ref, l_ref, acc_ref) = refs[2 * n_cast + 1:]
    for src_ref, dst_ref in zip(cast_src, cast_dst):
        dst_ref[...] = src_ref[...].astype(BF16)

    h = pl.program_id(1)
    qi = pl.program_id(2)
    n_kv = vt_ref.shape[0]
    kv_per_q = A_TQ // A_TK
    c = slope_ref[h] * LOG2E

    @pl.when(qi == 0)
    def _():
        rows = lax.broadcasted_iota(jnp.int32, (A_TK, A_TQ), 0)
        cols = lax.broadcasted_iota(jnp.int32, (A_TK, A_TQ), 1)
        cd = c * (rows - cols).astype(F32)
        tbl_ref[0] = cd
        tbl_ref[1] = -cd
        for d in range(kv_per_q):
            tbl_ref[2 + d] = -jnp.abs(cd + c * float(d * A_TK))

    qt_ref[...] = q_ref[...].astype(F32).T.astype(BF16)
    l_ref[...] = jnp.zeros(l_ref.shape, F32)
    acc_ref[...] = jnp.zeros(acc_ref.shape, F32)

    def bias_of(j):
        d = j - qi * kv_per_q
        off = c * (d * A_TK).astype(F32)
        idx = jnp.where(d < 0, 0, jnp.where(d >= kv_per_q, 1, 2 + d))
        kappa = jnp.where(d < 0, off, jnp.where(d >= kv_per_q, -off, 0.0))
        return idx, kappa

    def biased_logits(j, half, idx):
        k0 = pl.multiple_of(j * A_TK, A_TK)
        cols_h = slice(half * HEAD_DIM, (half + 1) * HEAD_DIM)
        return _dot(k_ref[pl.ds(k0, A_TK), cols_h], qt_ref[cols_h, :]) + tbl_ref[idx]

    bound = scal_ref[1]
    fixed_shift_ok = bound <= A_MAX_FIXED_SHIFT

    @pl.when(fixed_shift_ok)
    def _():
        def probabilities(j, p_ref, live):
            idx, kappa = bias_of(j)
            shift = kappa - bound
            for half in range(2):
                p = jnp.exp2(biased_logits(j, half, idx) + shift)
                l_ref[half] += live * jnp.sum(p, axis=0, keepdims=True)
                p_ref[half] = p.astype(BF16)

        def values(j, p_ref):
            vtb = vt_ref[j]
            for half in range(2):
                acc_ref[half] += _dot(vtb, p_ref[half])

        probabilities(0, p0_ref, 1.0)

        def pair(j):
            probabilities(j + 1, p1_ref, 1.0)
            values(j, p0_ref)
            nxt = j + 2
            probabilities(jnp.minimum(nxt, n_kv - 1), p0_ref, (nxt < n_kv).astype(F32))
            values(j + 1, p1_ref)

        def pairs(jj, carry):
            for u in range(A_PAIRS_PER_TRIP):
                pair(2 * (A_PAIRS_PER_TRIP * jj + u))
            return carry

        lax.fori_loop(0, n_kv // (2 * A_PAIRS_PER_TRIP), pairs, 0)

    @pl.when(jnp.logical_not(fixed_shift_ok))
    def _():
        m_ref[...] = jnp.full(m_ref.shape, NEG_BIG, F32)
        p1_ref[...] = jnp.zeros(p1_ref.shape, BF16)
        a1_ref[...] = jnp.ones(a1_ref.shape, F32)

        def logits(j, z_ref, zmax_ref):
            idx, _ = bias_of(j)
            for half in range(2):
                z = biased_logits(j, half, idx)
                z_ref[half] = z
                zmax_ref[half] = jnp.max(z, axis=0, keepdims=True)

        def softmax(j, z_ref, zmax_ref, p_ref, a_ref):
            _, kappa = bias_of(j)
            for half in range(2):
                m_old = m_ref[half]
                m_new = jnp.maximum(m_old, zmax_ref[half] + kappa)
                p = jnp.exp2(z_ref[half] - (m_new - kappa))
                alpha = jnp.exp2(m_old - m_new)
                l_ref[half] = alpha * l_ref[half] + jnp.sum(p, axis=0, keepdims=True)
                m_ref[half] = m_new
                a_ref[half] = alpha
                p_ref[half] = p.astype(BF16)

        def values(j, p_ref, a_ref):
            vtb = vt_ref[j]
            for half in range(2):
                acc_ref[half] = a_ref[half] * acc_ref[half] + _dot(vtb, p_ref[half])

        logits(0, s0_ref, x0_ref)

        def pair(jj, carry):
            j = 2 * jj
            logits(j + 1, s1_ref, x1_ref)
            softmax(j, s0_ref, x0_ref, p0_ref, a0_ref)
            values(jnp.maximum(j - 1, 0), p1_ref, a1_ref)
            logits(jnp.minimum(j + 2, n_kv - 1), s0_ref, x0_ref)
            softmax(j + 1, s1_ref, x1_ref, p1_ref, a1_ref)
            values(j, p0_ref, a0_ref)
            return carry

        lax.fori_loop(0, n_kv // 2, pair, 0)
        values(n_kv - 1, p1_ref, a1_ref)

    ot = acc_ref[0] * (1.0 / l_ref[0]) - scal_ref[0] * (acc_ref[1] * (1.0 / l_ref[1]))
    inv_rms = lax.rsqrt(jnp.mean(ot * ot, axis=0, keepdims=True) + RMS_EPS)
    ot = ot * inv_rms * (sg_ref[...] * out_scale)
    o_ref[...] = ot.T.astype(BF16)


def _logit_bound(q_gain, k_gain):
    q_scale = HEAD_DIM ** -0.5 * LOG2E
    return (HEAD_DIM * q_scale * 1.02) * jnp.max(jnp.abs(q_gain.astype(F32))) * jnp.max(jnp.abs(k_gain.astype(F32)))


def _cast_tiling(shape, n_steps):
    n_rows, width = math.prod(shape[:-1]), shape[-1]
    steps_per_tile = 1
    while n_rows * steps_per_tile % (n_steps * 2 * SUBLANE) != 0:
        steps_per_tile *= 2
    assert n_steps % steps_per_tile == 0 and width % LANE == 0
    return (n_rows * steps_per_tile // n_steps, width), steps_per_tile


def _diff_attention(qk, vt, lam, bound, subln_g, batch, seq, lambda_init, cast=()):
    assert A_TQ % A_TK == 0 and seq % A_TQ == 0
    assert (seq // A_TK) % (2 * A_PAIRS_PER_TRIP) == 0
    m = qk.shape[0]
    n_q = seq // A_TQ
    n_kv = seq // A_TK
    n_steps = batch * A_HEADS * n_q
    slopes = jnp.asarray(2.0 ** (-8.0 * np.arange(1, A_HEADS + 1) / A_HEADS), F32)
    kern = functools.partial(_diff_attn_kernel, out_scale=1.0 - lambda_init, n_cast=len(cast))
    smem = pl.BlockSpec(memory_space=pltpu.SMEM)
    def cast_spec(tile, steps_per_tile):
        return pl.BlockSpec(tile, lambda b, h, i: (((b * A_HEADS + h) * n_q + i) // steps_per_tile, 0))

    cast_specs = [cast_spec(*_cast_tiling(w.shape, n_steps)) for w in cast]
    cast_views = [w.reshape(-1, w.shape[-1]) for w in cast]
    out = pl.pallas_call(
        kern,
        grid=(batch, A_HEADS, n_q),
        in_specs=[
            smem,
            smem,
            pl.BlockSpec((A_TQ, A_VDIM), lambda b, h, i: (b * n_q + i, h)),
            pl.BlockSpec((seq, A_VDIM), lambda b, h, i: (b, A_HEADS + h)),
            pl.BlockSpec((n_kv, A_VDIM, A_TK), lambda b, h, i: (b, h, 0)),
            pl.BlockSpec((A_VDIM, 1), lambda b, h, i: (0, 0)),
        ] + cast_specs,
        out_specs=[pl.BlockSpec((A_TQ, A_VDIM), lambda b, h, i: (b * n_q + i, h))] + cast_specs,
        out_shape=[jax.ShapeDtypeStruct((m, A_HEADS * A_VDIM), BF16)]
        + [jax.ShapeDtypeStruct(v.shape, BF16) for v in cast_views],
        scratch_shapes=[
            pltpu.VMEM((A_VDIM, A_TQ), BF16),
            pltpu.VMEM((2 + A_TQ // A_TK, A_TK, A_TQ), F32),
            pltpu.VMEM((2, A_TK, A_TQ), F32),
            pltpu.VMEM((2, A_TK, A_TQ), F32),
            pltpu.VMEM((2, 1, A_TQ), F32),
            pltpu.VMEM((2, 1, A_TQ), F32),
            pltpu.VMEM((2, A_TK, A_TQ), BF16),
            pltpu.VMEM((2, A_TK, A_TQ), BF16),
            pltpu.VMEM((2, 1, A_TQ), F32),
            pltpu.VMEM((2, 1, A_TQ), F32),
            pltpu.VMEM((2, 1, A_TQ), F32),
            pltpu.VMEM((2, 1, A_TQ), F32),
            pltpu.VMEM((2, A_VDIM, A_TQ), F32),
        ],
        compiler_params=_params(("parallel", "parallel", "arbitrary")),
        name="diff_attention",
    )(jnp.stack([lam, bound]).astype(F32), slopes, qk, qk, vt, subln_g.reshape(A_VDIM, 1), *cast_views)
    return out[0], tuple(c.reshape(w.shape) for c, w in zip(out[1:], cast))


B_QROWS = 8
B_KROWS = 16
B_TQ = B_QROWS * GRID_W
B_TK = B_KROWS * GRID_W
B_PAIRS_PER_TRIP = 4


def _na_bias_table(rel_bias, rows):
    kr = min(NA_ROWS, rows)
    n_groups = rows // B_QROWS
    n_heads, n_drow, n_dcol = rel_bias.shape
    c = np.arange(GRID_W)
    c_start = np.clip(c - NA_COLS // 2, 0, GRID_W - NA_COLS)
    kc = np.arange(GRID_W)
    col_ok = (kc[None, :] >= c_start[:, None]) & (kc[None, :] < c_start[:, None] + NA_COLS)
    dcol = kc[None, :] - c[:, None] + (NA_COLS - 1)
    select = (np.arange(n_dcol)[:, None, None] == dcol[None]) & col_ok[None]
    cols = jnp.einsum("hrd,dck->hcrk", rel_bias * LOG2E, jnp.asarray(select, F32),
                      precision=lax.Precision.HIGHEST)
    strip = jnp.where(col_ok[None, :, None, :], cols, NEG_BIG).reshape(n_heads, GRID_W, n_drow * GRID_W)
    row_blocks = []
    for g in (0, 1, n_groups - 1):
        k_row0 = int(np.clip(g * B_QROWS - kr // 2, 0, rows - B_KROWS))
        for r in range(g * B_QROWS, (g + 1) * B_QROWS):
            r_start = int(np.clip(r - kr // 2, 0, rows - kr))
            lead = r_start - k_row0
            d0 = r_start - r + (NA_ROWS - 1)
            parts = [jnp.full((n_heads, GRID_W, lead * GRID_W), NEG_BIG, F32),
                     strip[:, :, d0 * GRID_W:(d0 + kr) * GRID_W],
                     jnp.full((n_heads, GRID_W, (B_KROWS - kr - lead) * GRID_W), NEG_BIG, F32)]
            row_blocks.append(jnp.concatenate(parts, axis=-1))
    return jnp.stack(row_blocks, axis=1).reshape(n_heads, 3, B_TQ, B_TK)


def _na_kernel(scal_ref, q_ref, k_ref, v_ref, bias_ref, o_ref, p0_ref, p1_ref, l0_ref, l1_ref, *, rows):
    n_groups = rows // B_QROWS
    shift, spread = scal_ref[0], scal_ref[1]

    def key_start(g):
        k_row0 = jnp.clip(g * B_QROWS - NA_ROWS // 2, 0, rows - B_KROWS)
        return pl.multiple_of(k_row0 * GRID_W, (NA_ROWS // 2) * GRID_W)

    def probabilities(g, p_ref, l_ref, fixed_shift):
        q0 = pl.multiple_of(g * B_TQ, B_TQ)
        variant = jnp.where(g == 0, 0, jnp.where(g == n_groups - 1, 2, 1))
        s = _dot_nt(q_ref[pl.ds(q0, B_TQ), :], k_ref[pl.ds(key_start(g), B_TK), :]) + bias_ref[0, variant]
        p = jnp.exp2(s - (shift if fixed_shift else jnp.max(s, axis=-1, keepdims=True)))
        l_ref[...] = jnp.sum(p, axis=-1, keepdims=True)
        p_ref[...] = p.astype(BF16)

    def values(g, p_ref, l_ref):
        q0 = pl.multiple_of(g * B_TQ, B_TQ)
        o = _dot(p_ref[...], v_ref[pl.ds(key_start(g), B_TK), :])
        o_ref[pl.ds(q0, B_TQ), :] = (o * (1.0 / l_ref[...])).astype(BF16)

    def pipeline(fixed_shift):
        probabilities(0, p0_ref, l0_ref, fixed_shift)

        def pair(g):
            probabilities(g + 1, p1_ref, l1_ref, fixed_shift)
            values(g, p0_ref, l0_ref)
            probabilities(jnp.minimum(g + 2, n_groups - 1), p0_ref, l0_ref, fixed_shift)
            values(g + 1, p1_ref, l1_ref)

        pairs_per_trip = B_PAIRS_PER_TRIP if fixed_shift else 1

        def pairs(gg, carry):
            for u in range(pairs_per_trip):
                pair(2 * (pairs_per_trip * gg + u))
            return carry

        lax.fori_loop(0, n_groups // (2 * pairs_per_trip), pairs, 0)

    fixed_shift_ok = spread <= 2.0 * A_MAX_FIXED_SHIFT
    pl.when(fixed_shift_ok)(lambda: pipeline(True))
    pl.when(jnp.logical_not(fixed_shift_ok))(lambda: pipeline(False))


def _neighbourhood_attention(qkv, bias, shift, spread, batch, seq):
    m = qkv.shape[0]
    rows = seq // GRID_W
    assert rows % (2 * B_PAIRS_PER_TRIP * B_QROWS) == 0 and rows >= B_KROWS and NA_ROWS <= rows
    kern = functools.partial(_na_kernel, rows=rows)
    return pl.pallas_call(
        kern,
        grid=(batch, B_HEADS),
        in_specs=[
            pl.BlockSpec(memory_space=pltpu.SMEM),
            pl.BlockSpec((seq, HEAD_DIM), lambda b, h: (b, h)),
            pl.BlockSpec((seq, HEAD_DIM), lambda b, h: (b, B_HEADS + h)),
            pl.BlockSpec((seq, HEAD_DIM), lambda b, h: (b, 2 * B_HEADS + h)),
            pl.BlockSpec((1, 3, B_TQ, B_TK), lambda b, h: (h, 0, 0, 0)),
        ],
        out_specs=pl.BlockSpec((seq, HEAD_DIM), lambda b, h: (b, h)),
        out_shape=jax.ShapeDtypeStruct((m, B_HEADS * HEAD_DIM), BF16),
        scratch_shapes=[
            pltpu.VMEM((B_TQ, B_TK), BF16),
            pltpu.VMEM((B_TQ, B_TK), BF16),
            pltpu.VMEM((B_TQ, 1), F32),
            pltpu.VMEM((B_TQ, 1), F32),
        ],
        compiler_params=_params(("parallel", "arbitrary")),
        name="neighbourhood_attention",
    )(jnp.stack([shift, spread]).astype(F32), qkv, qkv, qkv, bias)


C_TQ = 256
C_TK = C_TQ + 2 * C_WINDOW
C_PAIRS_PER_TRIP = 1
C_LEADS = 3


def _swa_kernel(scal_ref, slope_ref, sink_ref, q_ref, k_ref, v_ref, o_ref,
                tbl_ref, p0_ref, p1_ref, l0_ref, l1_ref):
    kv = pl.program_id(1)
    seq = k_ref.shape[0]
    n_tiles = seq // C_TQ
    bound, spread = scal_ref[0], scal_ref[1]

    rows = lax.broadcasted_iota(jnp.int32, (C_TQ, C_TK), 0)
    cols = lax.broadcasted_iota(jnp.int32, (C_TQ, C_TK), 1)
    for lead in range(C_LEADS):
        dist = jnp.abs(rows - cols + lead * C_WINDOW)
        tbl_ref[lead] = jnp.where(dist <= C_WINDOW, -dist.astype(F32), NEG_BIG)

    def key_start(t0):
        return pl.multiple_of(jnp.clip(t0 - C_WINDOW, 0, seq - C_TK), C_WINDOW)

    def probabilities(i, p_ref, l_ref, fixed_shift):
        t0 = pl.multiple_of(i * C_TQ, C_TQ)
        k0 = key_start(t0)
        lead = (t0 - k0) // C_WINDOW
        kb = k_ref[pl.ds(k0, C_TK), :]
        for g in range(C_GROUP):
            cols_g = slice(g * HEAD_DIM, (g + 1) * HEAD_DIM)
            slope = slope_ref[kv * C_GROUP + g] * LOG2E
            s = _dot_nt(q_ref[pl.ds(t0, C_TQ), cols_g], kb) + slope * tbl_ref[lead]
            sink = sink_ref[kv * C_GROUP + g] * LOG2E
            if fixed_shift:
                m = jnp.maximum(bound, sink)
            else:
                m = jnp.maximum(jnp.max(s, axis=-1, keepdims=True), sink)
            p = jnp.exp2(s - m)
            l_ref[g] = jnp.sum(p, axis=-1, keepdims=True) + jnp.exp2(sink - m)
            p_ref[g] = p.astype(BF16)

    def values(i, p_ref, l_ref):
        t0 = pl.multiple_of(i * C_TQ, C_TQ)
        vb = v_ref[pl.ds(key_start(t0), C_TK), :]
        for g in range(C_GROUP):
            cols_g = slice(g * HEAD_DIM, (g + 1) * HEAD_DIM)
            o_ref[pl.ds(t0, C_TQ), cols_g] = (_dot(p_ref[g], vb) * (1.0 / l_ref[g])).astype(BF16)

    def pipeline(fixed_shift):
        probabilities(0, p0_ref, l0_ref, fixed_shift)

        def pair(i):
            probabilities(i + 1, p1_ref, l1_ref, fixed_shift)
            values(i, p0_ref, l0_ref)
            probabilities(jnp.minimum(i + 2, n_tiles - 1), p0_ref, l0_ref, fixed_shift)
            values(i + 1, p1_ref, l1_ref)

        pairs_per_trip = C_PAIRS_PER_TRIP if fixed_shift else 1

        def pairs(ii, carry):
            for u in range(pairs_per_trip):
                pair(2 * (pairs_per_trip * ii + u))
            return carry

        lax.fori_loop(0, n_tiles // (2 * pairs_per_trip), pairs, 0)

    fixed_shift_ok = spread <= 2.0 * A_MAX_FIXED_SHIFT
    pl.when(fixed_shift_ok)(lambda: pipeline(True))
    pl.when(jnp.logical_not(fixed_shift_ok))(lambda: pipeline(False))


def _window_attention(qkv, sink, bound, batch, seq):
    m = qkv.shape[0]
    assert seq % (2 * C_PAIRS_PER_TRIP * C_TQ) == 0 and seq >= C_TK and C_TQ == (C_LEADS - 1) * C_WINDOW
    sink_max = jnp.max(sink.astype(F32)) * LOG2E
    scal = jnp.stack([bound, bound + jnp.maximum(bound, sink_max)]).astype(F32)
    slopes = jnp.asarray(2.0 ** (-8.0 * np.arange(1, C_HEADS + 1) / C_HEADS), F32)
    gw = C_GROUP * HEAD_DIM
    smem = pl.BlockSpec(memory_space=pltpu.SMEM)
    return pl.pallas_call(
        _swa_kernel,
        grid=(batch, C_KV_HEADS),
        in_specs=[
            smem,
            smem,
            smem,
            pl.BlockSpec((seq, gw), lambda b, kv: (b, kv)),
            pl.BlockSpec((seq, HEAD_DIM), lambda b, kv: (b, C_HEADS + kv)),
            pl.BlockSpec((seq, HEAD_DIM), lambda b, kv: (b, C_HEADS + C_KV_HEADS + kv)),
        ],
        out_specs=pl.BlockSpec((seq, gw), lambda b, kv: (b, kv)),
        out_shape=jax.ShapeDtypeStruct((m, C_HEADS * HEAD_DIM), BF16),
        scratch_shapes=[
            pltpu.VMEM((C_LEADS, C_TQ, C_TK), F32),
            pltpu.VMEM((C_GROUP, C_TQ, C_TK), BF16),
            pltpu.VMEM((C_GROUP, C_TQ, C_TK), BF16),
            pltpu.VMEM((C_GROUP, C_TQ, 1), F32),
            pltpu.VMEM((C_GROUP, C_TQ, 1), F32),
        ],
        compiler_params=_params(("parallel", "arbitrary")),
        name="window_attention",
    )(scal, slopes, sink.astype(F32), qkv, qkv, qkv)


def _col_gain(q_gain, k_gain, n_q_heads, n_k_heads, n_cols):
    q_scale = HEAD_DIM ** -0.5 * LOG2E
    parts = [jnp.tile(q_gain.astype(F32) * q_scale, n_q_heads), jnp.tile(k_gain.astype(F32), n_k_heads)]
    gain = jnp.concatenate(parts)
    return jnp.pad(gain, (0, n_cols - gain.shape[0]), constant_values=1.0).reshape(1, n_cols)


def kernel(x, norm_ffn1, ffn1_w_gate, ffn1_w_up, ffn1_w_down, norm_mix, norm_ffn2, ffn2_w_gate, ffn2_w_up, ffn2_w_down, a_w_qkv, a_q_norm, a_k_norm, a_lambda_q1, a_lambda_k1, a_lambda_q2, a_lambda_k2, a_subln, a_w_o, b_w_qkv, b_q_norm, b_k_norm, b_rel_bias, b_w_o, c_w_qkv, c_q_norm, c_k_norm, c_sink, c_w_o):
    batch, seq, d = x.shape
    x = x.reshape(batch * seq, d)
    first_slab = lambda w: w[:1].astype(BF16)
    ffn1 = tuple(map(first_slab, (ffn1_w_gate, ffn1_w_up, ffn1_w_down)))
    ffn2 = None
    later_f32 = (ffn1_w_gate, ffn1_w_up, ffn1_w_down, ffn2_w_gate, ffn2_w_up, ffn2_w_down,
                 a_w_qkv, a_w_o, b_w_qkv, b_w_o, c_w_qkv, c_w_o)
    a_w_qkv, a_w_o = first_slab(a_w_qkv), first_slab(a_w_o)

    for i in range(DEPTH):
        x = _ffn(x, norm_ffn1[i], *ffn1, i)
        kind, j = i % N_MIXERS, i // N_MIXERS
        if kind == 0:
            lambda_init = 0.8 - 0.6 * math.exp(-0.3 * i)
            lam = (jnp.exp(jnp.sum(a_lambda_q1[j].astype(F32) * a_lambda_k1[j].astype(F32)))
                   - jnp.exp(jnp.sum(a_lambda_q2[j].astype(F32) * a_lambda_k2[j].astype(F32))) + lambda_init)
            n_qk = 2 * A_HEADS * 2 * HEAD_DIM
            gain = _col_gain(a_q_norm[j], a_k_norm[j], 2 * A_HEADS, 2 * A_HEADS, n_qk)
            qk = _qkv(x, norm_mix[i], a_w_qkv, j, gain, n_qk, n=n_qk, sub_blocks=QKV_SUB_BLOCKS)
            vt = _proj_t(x, norm_mix[i], a_w_qkv, j, n_qk)
            bound = _logit_bound(a_q_norm[j], a_k_norm[j])
            o, cast = _diff_attention(qk, vt, lam, bound, a_subln[j], batch, seq, lambda_init,
                                      cast=later_f32 if i == 0 else ())
            x = _out_proj(o, a_w_o, j, x)
            if i == 0:
                ffn1, ffn2 = cast[:3], cast[3:6]
                a_w_qkv, a_w_o, b_w_qkv, b_w_o, c_w_qkv, c_w_o = cast[6:]
        elif kind == 1:
            n_qk = 2 * B_HEADS * HEAD_DIM
            gain = _col_gain(b_q_norm[j], b_k_norm[j], B_HEADS, B_HEADS, b_w_qkv.shape[2])
            qkv = _qkv(x, norm_mix[i], b_w_qkv, j, gain, n_qk, sub_blocks=QKV_SUB_BLOCKS)
            rel_bias = b_rel_bias[j].astype(F32)
            bias = _na_bias_table(rel_bias, seq // GRID_W)
            bound = _logit_bound(b_q_norm[j], b_k_norm[j])
            bias_max, bias_min = jnp.max(rel_bias) * LOG2E, jnp.min(rel_bias) * LOG2E
            o = _neighbourhood_attention(qkv, bias, bound + bias_max, 2.0 * bound + bias_max - bias_min,
                                         batch, seq)
            x = _out_proj(o, b_w_o, j, x)
        else:
            n_qk = (C_HEADS + C_KV_HEADS) * HEAD_DIM
            gain = _col_gain(c_q_norm[j], c_k_norm[j], C_HEADS, C_KV_HEADS, c_w_qkv.shape[2])
            qkv = _qkv(x, norm_mix[i], c_w_qkv, j, gain, n_qk)
            o = _window_attention(qkv, c_sink[j], _logit_bound(c_q_norm[j], c_k_norm[j]), batch, seq)
            x = _out_proj(o, c_w_o, j, x)
        x = _ffn(x, norm_ffn2[i], *ffn2, i)
    return x.reshape(batch, seq, d)
```
